```python
import math
import jax, jax.numpy as jnp
from jax import lax
import numpy as np

D_MODEL = 2048
BATCH = 1
SEQ = 16384
DEPTH = 2

GLA_HEADS = 4
GLA_DK = 128
GLA_DV = 256
GLA_GATE_RANK = 16
GLA_GATE_NORMALIZER = 16.0
GLA_CHUNK = 64
MOBA_HEADS = 8
MOBA_HD = 128
MOBA_BLOCK = 256
MOBA_TOPK = 3
MOBA_QCHUNK = 64
ROPE_THETA = 10000.0
GLA_QK_W = GLA_HEADS * GLA_DK
GLA_V_W = GLA_HEADS * GLA_DV
MOBA_W = MOBA_HEADS * MOBA_HD
MIX_W = GLA_V_W + MOBA_W
IN_SIZES = [GLA_QK_W, GLA_QK_W, GLA_V_W, GLA_V_W, GLA_GATE_RANK, MOBA_W, MOBA_W, MOBA_W]
D_IN = sum(IN_SIZES)
IN_SPLITS = [int(s) for s in np.cumsum(IN_SIZES)[:-1]]
N_EXPERTS = 16
N_GROUPS = 4
EXPERTS_PER_GROUP = N_EXPERTS // N_GROUPS
TOP_K = 2
D_EXPERT = 1024
DN_ALPHA = (2.0 * DEPTH) ** 0.25
DN_BETA = (8.0 * DEPTH) ** -0.25
LN_EPS = 1e-5
RMS_EPS = 1e-6
NEG = -1e30

kernel_name = "hybrid_gla_moba_shared_router_moe_deepnorm"


def layer_norm(x, g, b):
    xf = x.astype(jnp.float32)
    mu = jnp.mean(xf, axis=-1, keepdims=True)
    var = jnp.mean(jnp.square(xf - mu), axis=-1, keepdims=True)
    y = (xf - mu) * lax.rsqrt(var + LN_EPS) * g.astype(jnp.float32) + b.astype(jnp.float32)
    return y.astype(x.dtype)


def rope_tables(positions):
    inv = 1.0 / (ROPE_THETA ** (jnp.arange(0, MOBA_HD, 2, dtype=jnp.float32) / MOBA_HD))
    ang = positions.astype(jnp.float32)[..., None] * inv
    return jnp.cos(ang), jnp.sin(ang)


def apply_rope(x, cos, sin):
    xf = x.astype(jnp.float32)
    x1, x2 = jnp.split(xf, 2, axis=-1)
    c, s = cos[:, :, None, :], sin[:, :, None, :]
    return jnp.concatenate([x1 * c - x2 * s, x2 * c + x1 * s], axis=-1).astype(x.dtype)


def gla_chunked(q, k, v, log_decay):
    B, S, H, DK = q.shape
    DV = v.shape[-1]
    C = GLA_CHUNK
    nc = S // C

    def to_chunks(t):
        return t.astype(jnp.float32).reshape(B, nc, C, H, t.shape[-1]).transpose(1, 0, 3, 2, 4)

    qc = to_chunks(q) * (DK ** -0.5)
    kc, vc, gc = to_chunks(k), to_chunks(v), to_chunks(log_decay)
    causal = jnp.tril(jnp.ones((C, C), dtype=bool))

    def step(state, inp):
        qi, ki, vi, gi = inp
        b = jnp.cumsum(gi, axis=2)
        inter = jnp.einsum('bhcd,bhde->bhce', qi * jnp.exp(b), state)
        diff = b[:, :, :, None, :] - b[:, :, None, :, :]
        decay = jnp.exp(jnp.where(causal[None, None, :, :, None], diff, -jnp.inf))
        attn = jnp.einsum('bhid,bhjd,bhijd->bhij', qi, ki, decay)
        intra = jnp.einsum('bhij,bhje->bhie', attn, vi)
        b_last = b[:, :, -1:, :]
        new_state = jnp.exp(b_last[:, :, 0, :])[..., None] * state + jnp.einsum(
            'bhcd,bhce->bhde', ki * jnp.exp(b_last - b), vi)
        return new_state, inter + intra

    state0 = jnp.zeros((B, H, DK, DV), jnp.float32)
    _, o = lax.scan(step, state0, (qc, kc, vc, gc))
    return o.transpose(1, 0, 3, 2, 4).reshape(B, S, H, DV)


def moba_attention(q, k, v):
    B, S, H, D = q.shape
    nb = -(-S // MOBA_BLOCK)
    Sp = nb * MOBA_BLOCK
    pad = Sp - S

    def prep(t):
        t = jnp.pad(t, ((0, 0), (0, pad), (0, 0), (0, 0)))
        return t.transpose(0, 2, 1, 3).reshape(B * H, Sp, D)

    q, k, v = prep(q), prep(k), prep(v)
    BH = B * H
    kb = k.reshape(BH, nb, MOBA_BLOCK, D)
    vb = v.reshape(BH, nb, MOBA_BLOCK, D)
    k_mean = jnp.mean(kb.astype(jnp.float32), axis=2)
    gate = jnp.einsum('nsd,nbd->nsb', q.astype(jnp.float32), k_mean)
    q_blk = jnp.arange(Sp) // MOBA_BLOCK
    fully_past = jnp.arange(nb)[None, :] < q_blk[:, None]
    gate = jnp.where(fully_past[None], gate, NEG)
    kk = min(MOBA_TOPK, nb)
    _, sel = lax.top_k(gate, kk)
    sel_valid = jnp.arange(kk)[None, :] < q_blk[:, None]
    scale = D ** -0.5
    QC = MOBA_QCHUNK

    def chunk(c):
        start = c * QC
        qc = lax.dynamic_slice_in_dim(q, start, QC, axis=1)
        selc = lax.dynamic_slice_in_dim(sel, start, QC, axis=1)
        validc = lax.dynamic_slice_in_dim(sel_valid, start, QC, axis=0)
        own = start // MOBA_BLOCK
        k_own = lax.dynamic_index_in_dim(kb, own, axis=1, keepdims=False)
        v_own = lax.dynamic_index_in_dim(vb, own, axis=1, keepdims=False)
        k_sel = jax.vmap(lambda blocks, idx: blocks[idx])(kb, selc)
        v_sel = jax.vmap(lambda blocks, idx: blocks[idx])(vb, selc)
        s_sel = jnp.einsum('nqd,nqjkd->nqjk', qc, k_sel).astype(jnp.float32) * scale
        s_sel = jnp.where(validc[None, :, :, None], s_sel, NEG).reshape(BH, QC, kk * MOBA_BLOCK)
        s_own = jnp.einsum('nqd,nkd->nqk', qc, k_own).astype(jnp.float32) * scale
        q_pos = start + jnp.arange(QC)
        k_pos = own * MOBA_BLOCK + jnp.arange(MOBA_BLOCK)
        s_own = jnp.where((k_pos[None, :] <= q_pos[:, None])[None], s_own, NEG)
        p = jax.nn.softmax(jnp.concatenate([s_sel, s_own], axis=-1), axis=-1).astype(v.dtype)
        p_sel, p_own = p[..., :kk * MOBA_BLOCK], p[..., kk * MOBA_BLOCK:]
        o = jnp.einsum('nqk,nqkd->nqd', p_sel, v_sel.reshape(BH, QC, kk * MOBA_BLOCK, D))
        return o + jnp.einsum('nqk,nkd->nqd', p_own, v_own)

    out = lax.map(chunk, jnp.arange(Sp // QC))
    out = out.transpose(1, 0, 2, 3).reshape(B, H, Sp, D)[:, :, :S]
    return out.transpose(0, 2, 1, 3)


def mixer(h, w_in, w_gk, b_gk, gla_norm, w_o, cos, sin):
    B, S, _ = h.shape
    proj = h @ w_in
    gq, gk, gv, gg, glow, mq, mk, mv = jnp.split(proj, IN_SPLITS, axis=-1)
    log_decay = jax.nn.log_sigmoid((glow @ w_gk + b_gk).astype(jnp.float32)) / GLA_GATE_NORMALIZER
    o_gla = gla_chunked(gq.reshape(B, S, GLA_HEADS, GLA_DK), gk.reshape(B, S, GLA_HEADS, GLA_DK),
                        gv.reshape(B, S, GLA_HEADS, GLA_DV), log_decay.reshape(B, S, GLA_HEADS, GLA_DK))
    o_gla = o_gla * lax.rsqrt(jnp.mean(jnp.square(o_gla), axis=-1, keepdims=True) + RMS_EPS)
    o_gla = o_gla * gla_norm.astype(jnp.float32) * jax.nn.silu(
        gg.astype(jnp.float32).reshape(B, S, GLA_HEADS, GLA_DV))
    o_gla = o_gla.reshape(B, S, GLA_V_W).astype(h.dtype)
    mq = apply_rope(mq.reshape(B, S, MOBA_HEADS, MOBA_HD), cos, sin)
    mk = apply_rope(mk.reshape(B, S, MOBA_HEADS, MOBA_HD), cos, sin)
    o_moba = moba_attention(mq, mk, mv.reshape(B, S, MOBA_HEADS, MOBA_HD)).reshape(B, S, MOBA_W)
    return jnp.concatenate([o_gla, o_moba.astype(h.dtype)], axis=-1) @ w_o


def moe(h, w_router, b_router, w_gate, w_up, w_down):
    B, S, D = h.shape
    t = h.reshape(-1, D)
    aff = jax.nn.softmax((t @ w_router).astype(jnp.float32), axis=-1)
    biased = aff + b_router.astype(jnp.float32)
    T = t.shape[0]
    group_score = jnp.sum(lax.top_k(biased.reshape(T, N_GROUPS, EXPERTS_PER_GROUP), TOP_K)[0], axis=-1)
    best_group = jnp.argmax(group_score, axis=-1)
    in_group = (jnp.arange(N_EXPERTS) // EXPERTS_PER_GROUP)[None, :] == best_group[:, None]
    _, idx = lax.top_k(jnp.where(in_group, biased, -jnp.inf), TOP_K)
    w = jnp.take_along_axis(aff, idx, axis=-1)
    w = w / jnp.sum(w, axis=-1, keepdims=True)
    combine = jnp.sum(jax.nn.one_hot(idx, N_EXPERTS, dtype=jnp.float32) * w[..., None], axis=1)
    y = jnp.zeros((T, D), jnp.float32)
    for e in range(N_EXPERTS):
        he = jax.nn.silu(t @ w_gate[e]) * (t @ w_up[e])
        y = y + combine[:, e:e + 1] * (he @ w_down[e]).astype(jnp.float32)
    return y.astype(h.dtype).reshape(B, S, D)


def setup_inputs(seed: int = 0) -> dict:
    key = jax.random.key(seed)
    ks = jax.random.split(key, 16)
    nrm = jax.random.normal
    x = nrm(ks[0], (BATCH, SEQ, D_MODEL), jnp.float32)
    positions = jnp.broadcast_to(jnp.arange(SEQ, dtype=jnp.int32)[None, :], (BATCH, SEQ))
    w_in = nrm(ks[1], (DEPTH, D_MODEL, D_IN), jnp.float32) * D_MODEL ** -0.5
    w_gk = nrm(ks[2], (DEPTH, GLA_GATE_RANK, GLA_QK_W), jnp.float32) * GLA_GATE_RANK ** -0.5
    b_gk = nrm(ks[3], (DEPTH, GLA_QK_W), jnp.float32) * 0.02
    gla_norm = 1.0 + 0.02 * nrm(ks[4], (DEPTH, GLA_DV), jnp.float32)
    w_o = nrm(ks[5], (DEPTH, MIX_W, D_MODEL), jnp.float32) * (MIX_W ** -0.5) * DN_BETA
    ln1_g = 1.0 + 0.02 * nrm(ks[6], (DEPTH, D_MODEL), jnp.float32)
    ln1_b = 0.02 * nrm(ks[7], (DEPTH, D_MODEL), jnp.float32)
    w_router = nrm(ks[8], (D_MODEL, N_EXPERTS), jnp.float32) * D_MODEL ** -0.5
    b_router = 0.01 * nrm(ks[9], (N_EXPERTS,), jnp.float32)
    w_gate = nrm(ks[10], (DEPTH, N_EXPERTS, D_MODEL, D_EXPERT), jnp.float32) * D_MODEL ** -0.5
    w_up = nrm(ks[11], (DEPTH, N_EXPERTS, D_MODEL, D_EXPERT), jnp.float32) * D_MODEL ** -0.5
    w_down = nrm(ks[12], (DEPTH, N_EXPERTS, D_EXPERT, D_MODEL), jnp.float32) * (D_EXPERT ** -0.5) * DN_BETA
    ln2_g = 1.0 + 0.02 * nrm(ks[13], (DEPTH, D_MODEL), jnp.float32)
    ln2_b = 0.02 * nrm(ks[14], (DEPTH, D_MODEL), jnp.float32)
    return {"x": x, "positions": positions, "w_in": w_in, "w_gk": w_gk, "b_gk": b_gk,
            "gla_norm": gla_norm, "w_o": w_o, "ln1_g": ln1_g, "ln1_b": ln1_b,
            "w_router": w_router, "b_router": b_router, "w_gate": w_gate, "w_up": w_up,
            "w_down": w_down, "ln2_g": ln2_g, "ln2_b": ln2_b}


def reference(x, positions, w_in, w_gk, b_gk, gla_norm, w_o, ln1_g, ln1_b,
              w_router, b_router, w_gate, w_up, w_down, ln2_g, ln2_b):
    cos, sin = rope_tables(positions)
    h = x
    for l in range(DEPTH):
        h = layer_norm(DN_ALPHA * h + mixer(h, w_in[l], w_gk[l], b_gk[l], gla_norm[l], w_o[l], cos, sin),
                       ln1_g[l], ln1_b[l])
        h = layer_norm(DN_ALPHA * h + moe(h, w_router, b_router, w_gate[l], w_up[l], w_down[l]),
                       ln2_g[l], ln2_b[l])
    return h
```

```python
import functools
import math

import jax
import jax.numpy as jnp
from jax import lax
from jax.experimental import pallas as pl
from jax.experimental.pallas import tpu as pltpu

D_MODEL = 2048
DEPTH = 2
GLA_HEADS = 4
GLA_DK = 128
GLA_DV = 256
GLA_GATE_RANK = 16
GLA_GATE_NORMALIZER = 16.0
MOBA_HEADS = 8
MOBA_HD = 128
MOBA_BLOCK = 256
MOBA_TOPK = 3
ROPE_THETA = 10000.0
GLA_QK_W = GLA_HEADS * GLA_DK
GLA_V_W = GLA_HEADS * GLA_DV
MOBA_W = MOBA_HEADS * MOBA_HD
N_EXPERTS = 16
N_GROUPS = 4
EXPERTS_PER_GROUP = N_EXPERTS // N_GROUPS
D_EXPERT = 1024
DN_ALPHA = (2.0 * DEPTH) ** 0.25
LN_EPS = 1e-5
RMS_EPS = 1e-6
NEG = -1e30

LANES = 128
GLA_CHUNK = 64
GLA_SUB = 16
VMEM_LIMIT = 48 * 1024 * 1024

PROJ_MAIN_W = 2 * GLA_QK_W + 2 * GLA_V_W + 3 * MOBA_W
PROJ_W = PROJ_MAIN_W + LANES
LOW_OFF = 2 * GLA_QK_W + 2 * GLA_V_W
PROJ_TN = 896
COL_GQ = 0
COL_GK = GLA_QK_W // GLA_DK
COL_GV = (2 * GLA_QK_W) // GLA_DV
COL_GG = (2 * GLA_QK_W + GLA_V_W) // GLA_DV
COL_MQ = LOW_OFF // MOBA_HD
COL_MK = COL_MQ + MOBA_HEADS
COL_MV = COL_MK + MOBA_HEADS
COL_LOW = PROJ_MAIN_W // LANES


def _cparams(semantics):
    return pltpu.CompilerParams(dimension_semantics=semantics, vmem_limit_bytes=VMEM_LIMIT)


def _split_bf16(x):
    hi = x.astype(jnp.bfloat16)
    lo = (x - hi.astype(jnp.float32)).astype(jnp.bfloat16)
    return hi, lo


def _dot(a, b):
    return jnp.dot(a, b, preferred_element_type=jnp.float32)


def _dot_nt(a, b):
    return lax.dot_general(a, b, (((1,), (1,)), ((), ())), preferred_element_type=jnp.float32)


def _layer_norm(y, g, b):
    mu = jnp.mean(y, axis=-1, keepdims=True)
    yc = y - mu
    var = jnp.mean(yc * yc, axis=-1, keepdims=True)
    return yc * lax.rsqrt(var + LN_EPS) * g + b


def _silu(x):
    return x * (1.0 / (1.0 + jnp.exp(-x)))


def _proj_kernel(x_ref, w_ref, o_ref):
    o_ref[...] = _dot(x_ref[...], w_ref[...]).astype(o_ref.dtype)


def _in_projection(h_bf16, w_packed, tm=512):
    t = h_bf16.shape[0]
    return pl.pallas_call(
        _proj_kernel,
        grid=(PROJ_W // PROJ_TN, t // tm),
        in_specs=[pl.BlockSpec((tm, D_MODEL), lambda j, i: (i, 0)),
                  pl.BlockSpec((D_MODEL, PROJ_TN), lambda j, i: (0, j))],
        out_specs=pl.BlockSpec((tm, PROJ_TN), lambda j, i: (i, j)),
        out_shape=jax.ShapeDtypeStruct((t, PROJ_W), jnp.bfloat16),
        compiler_params=_cparams(("parallel", "parallel")),
        name="in_projection",
    )(h_bf16, w_packed)


def _gla_kernel(q_ref, k_ref, v_ref, gg_ref, low_ref, wgk_ref, bgk_ref, norm_ref, o_ref, st_ref, *, n_chunks):
    c_len, sub = GLA_CHUNK, GLA_SUB
    n_sub = c_len // sub

    @pl.when(pl.program_id(1) == 0)
    def _():
        st_ref[...] = jnp.zeros_like(st_ref)

    row = lax.broadcasted_iota(jnp.int32, (c_len, c_len), 0)
    col = lax.broadcasted_iota(jnp.int32, (c_len, c_len), 1)
    tril = jnp.where(col <= row, 1.0, 0.0).astype(jnp.bfloat16)
    sub_row = lax.broadcasted_iota(jnp.int32, (sub, GLA_DK), 0)
    lane_c = lax.broadcasted_iota(jnp.int32, (sub, c_len), 1)
    wgk = wgk_ref[...]
    bgk = bgk_ref[...]
    gnorm = norm_ref[...]

    def chunk(c, carry):
        r0 = pl.multiple_of(c * c_len, c_len)
        q = q_ref[pl.ds(r0, c_len), :].astype(jnp.float32) * (GLA_DK ** -0.5)
        k = k_ref[pl.ds(r0, c_len), :].astype(jnp.float32)
        v = v_ref[pl.ds(r0, c_len), :]
        x = _dot(low_ref[pl.ds(r0, c_len), :], wgk) + bgk
        g = (jnp.minimum(x, 0.0) - jnp.log1p(jnp.exp(-jnp.abs(x)))) * (1.0 / GLA_GATE_NORMALIZER)
        g_hi, g_lo = _split_bf16(g)
        b = _dot(tril, g_hi) + _dot(tril, g_lo)
        b_last = b[c_len - 1:c_len, :]

        st = st_ref[...]
        inter = _dot_nt((q * jnp.exp(b)).astype(jnp.bfloat16), st.astype(jnp.bfloat16))
        k_dec = (k * jnp.exp(b_last - b)).astype(jnp.bfloat16)
        st_ref[...] = st * jnp.exp(b_last) + _dot(v.T, k_dec)

        blocks = []
        for i_sub in range(n_sub):
            lo = i_sub * sub
            q_i, k_i, b_i = q[lo:lo + sub], k[lo:lo + sub], b[lo:lo + sub]
            a_i = jnp.zeros((sub, c_len), jnp.float32)
            for j in range(sub):
                e = jnp.exp(jnp.minimum(b_i - b_i[j:j + 1, :], 0.0))
                term = jnp.where(sub_row >= j, q_i * e * k_i[j:j + 1, :], 0.0)
                a_i = jnp.where(lane_c == lo + j, jnp.sum(term, axis=1, keepdims=True), a_i)
            if i_sub > 0:
                b_start = b[lo - 1:lo, :]
                q_s = (q_i * jnp.exp(b_i - b_start)).astype(jnp.bfloat16)
                k_s = (k * jnp.exp(jnp.minimum(b_start - b, 0.0))).astype(jnp.bfloat16)
                a_i = jnp.where(lane_c < lo, _dot_nt(q_s, k_s), a_i)
            blocks.append(a_i)
        attn = jnp.concatenate(blocks, axis=0).astype(jnp.bfloat16)
        o = inter + _dot(attn, v)

        o = o * lax.rsqrt(jnp.mean(o * o, axis=-1, keepdims=True) + RMS_EPS)
        gate = gg_ref[pl.ds(r0, c_len), :].astype(jnp.float32)
        o_ref[pl.ds(r0, c_len), :] = (o * gnorm * _silu(gate)).astype(o_ref.dtype)
        return carry

    lax.fori_loop(0, n_chunks, chunk, 0)


def _gla(proj, wgk_pad, b_gk, gla_norm, bt=512):
    t = proj.shape[0]
    kern = functools.partial(_gla_kernel, n_chunks=bt // GLA_CHUNK)
    return pl.pallas_call(
        kern,
        grid=(GLA_HEADS, t // bt),
        in_specs=[pl.BlockSpec((bt, GLA_DK), lambda h, i: (i, COL_GQ + h)),
                  pl.BlockSpec((bt, GLA_DK), lambda h, i: (i, COL_GK + h)),
                  pl.BlockSpec((bt, GLA_DV), lambda h, i: (i, COL_GV + h)),
                  pl.BlockSpec((bt, GLA_DV), lambda h, i: (i, COL_GG + h)),
                  pl.BlockSpec((bt, LANES), lambda h, i: (i, COL_LOW)),
                  pl.BlockSpec((LANES, GLA_DK), lambda h, i: (0, h)),
                  pl.BlockSpec((1, GLA_DK), lambda h, i: (0, h)),
                  pl.BlockSpec((1, GLA_DV), lambda h, i: (0, 0))],
        out_specs=pl.BlockSpec((bt, GLA_DV), lambda h, i: (i, h)),
        out_shape=jax.ShapeDtypeStruct((t, GLA_V_W), jnp.bfloat16),
        scratch_shapes=[pltpu.VMEM((GLA_DV, GLA_DK), jnp.float32)],
        compiler_params=_cparams(("parallel", "arbitrary")),
        name="gla",
    )(proj, proj, proj, proj, proj, wgk_pad, b_gk, gla_norm)


def _rope(x, cos_full, sin_signed):
    return x * cos_full + pltpu.roll(x, MOBA_HD // 2, axis=1) * sin_signed


def _moba_prep_kernel(q_ref, k_ref, cos_ref, sin_ref, qo_ref, ko_ref, km_ref):
    cos_full, sin_signed = cos_ref[...], sin_ref[...]
    q = _rope(q_ref[...].astype(jnp.float32), cos_full, sin_signed)
    k = _rope(k_ref[...].astype(jnp.float32), cos_full, sin_signed)
    qo_ref[0] = (q * (MOBA_HD ** -0.5)).astype(qo_ref.dtype)
    ko_ref[0] = k.astype(ko_ref.dtype)
    km_ref[0, 0] = jnp.mean(k, axis=0, keepdims=True)


def _moba_prep(proj, cos_full, sin_signed):
    t = proj.shape[0]
    nb = t // MOBA_BLOCK
    blk = pl.BlockSpec((1, MOBA_BLOCK, MOBA_HD), lambda h, i: (h, i, 0))
    return pl.pallas_call(
        _moba_prep_kernel,
        grid=(MOBA_HEADS, nb),
        in_specs=[pl.BlockSpec((MOBA_BLOCK, MOBA_HD), lambda h, i: (i, COL_MQ + h)),
                  pl.BlockSpec((MOBA_BLOCK, MOBA_HD), lambda h, i: (i, COL_MK + h)),
                  pl.BlockSpec((MOBA_BLOCK, MOBA_HD), lambda h, i: (i, 0)),
                  pl.BlockSpec((MOBA_BLOCK, MOBA_HD), lambda h, i: (i, 0))],
        out_specs=[blk, blk, pl.BlockSpec((1, 1, 1, MOBA_HD), lambda h, i: (h, i, 0, 0))],
        out_shape=[jax.ShapeDtypeStruct((MOBA_HEADS, t, MOBA_HD), jnp.bfloat16),
                   jax.ShapeDtypeStruct((MOBA_HEADS, t, MOBA_HD), jnp.bfloat16),
                   jax.ShapeDtypeStruct((MOBA_HEADS, nb, 1, MOBA_HD), jnp.float32)],
        compiler_params=_cparams(("parallel", "parallel")),
        name="moba_prep",
    )(proj, proj, cos_full, sin_signed)


def _moba_kernel(q_ref, k_ref, v_ref, km_ref, o_ref, *, nb):
    blk = MOBA_BLOCK
    i = pl.program_id(1)
    q = q_ref[0]

    km_hi, km_lo = _split_bf16(km_ref[0])
    gate = _dot_nt(q, km_hi) + _dot_nt(q, km_lo)
    lane = lax.broadcasted_iota(jnp.int32, (blk, nb), 1).astype(jnp.float32)
    gate = jnp.where(lane < i.astype(jnp.float32), gate, NEG)
    sels = []
    for _ in range(MOBA_TOPK):
        m = jnp.max(gate, axis=1, keepdims=True)
        idx = jnp.min(jnp.where(gate == m, lane, float(nb)), axis=1, keepdims=True)
        sels.append(jnp.where(m > 0.5 * NEG, idx, -1.0))
        gate = jnp.where(lane == idx, NEG, gate)

    r0 = pl.multiple_of(i * blk, blk)
    s = _dot_nt(q, k_ref[0, pl.ds(r0, blk), :])
    row = lax.broadcasted_iota(jnp.int32, (blk, blk), 0)
    col = lax.broadcasted_iota(jnp.int32, (blk, blk), 1)
    s = jnp.where(col <= row, s, NEG)
    m0 = jnp.max(s, axis=1, keepdims=True)
    p = jnp.exp(s - m0)
    l0 = jnp.sum(p, axis=1, keepdims=True)
    acc0 = _dot(p.astype(jnp.bfloat16), v_ref[pl.ds(r0, blk), :])

    def past_block(j, carry):
        m_run, l_run, acc = carry
        jf = j.astype(jnp.float32)
        picked = (sels[0] == jf) | (sels[1] == jf) | (sels[2] == jf)
        bias = jnp.where(picked, 0.0, NEG)
        c0 = pl.multiple_of(j * blk, blk)
        s_j = _dot_nt(q, k_ref[0, pl.ds(c0, blk), :]) + bias
        m_new = jnp.maximum(m_run, jnp.max(s_j, axis=1, keepdims=True))
        alpha = jnp.exp(m_run - m_new)
        p_j = jnp.exp(s_j - m_new)
        l_new = alpha * l_run + jnp.sum(p_j, axis=1, keepdims=True)
        acc_new = alpha * acc + _dot(p_j.astype(jnp.bfloat16), v_ref[pl.ds(c0, blk), :])
        return m_new, l_new, acc_new

    _, l_fin, acc_fin = lax.fori_loop(0, i, past_block, (m0, l0, acc0))
    o_ref[...] = (acc_fin / l_fin).astype(o_ref.dtype)


def _moba(q_rot, k_rot, proj, kmean):
    t = proj.shape[0]
    nb = t // MOBA_BLOCK
    return pl.pallas_call(
        functools.partial(_moba_kernel, nb=nb),
        grid=(MOBA_HEADS, nb),
        in_specs=[pl.BlockSpec((1, MOBA_BLOCK, MOBA_HD), lambda h, i: (h, i, 0)),
                  pl.BlockSpec((1, t, MOBA_HD), lambda h, i: (h, 0, 0)),
                  pl.BlockSpec((t, MOBA_HD), lambda h, i: (0, COL_MV + h)),
                  pl.BlockSpec((1, nb, MOBA_HD), lambda h, i: (h, 0, 0))],
        out_specs=pl.BlockSpec((MOBA_BLOCK, MOBA_HD), lambda h, i: (i, h)),
        out_shape=jax.ShapeDtypeStruct((t, MOBA_W), jnp.bfloat16),
        compiler_params=_cparams(("parallel", "arbitrary")),
        name="moba_attention",
    )(q_rot, k_rot, proj, kmean)


def _route_rows(logit_rows, bias_ref):
    mx = functools.reduce(jnp.maximum, logit_rows)
    ex = [jnp.exp(l - mx) for l in logit_rows]
    inv_den = 1.0 / functools.reduce(lambda a, b: a + b, ex)
    aff = [e * inv_den for e in ex]
    biased = [a + bias_ref[e] for e, a in enumerate(aff)]

    def top2_sum(a0, a1, a2, a3):
        hi01, lo01 = jnp.maximum(a0, a1), jnp.minimum(a0, a1)
        hi23, lo23 = jnp.maximum(a2, a3), jnp.minimum(a2, a3)
        return jnp.maximum(hi01, hi23) + jnp.maximum(jnp.minimum(hi01, hi23), jnp.maximum(lo01, lo23))

    g = EXPERTS_PER_GROUP
    scores = [top2_sum(*biased[gi * g:(gi + 1) * g]) for gi in range(N_GROUPS)]
    best, best_score = jnp.zeros_like(scores[0]), scores[0]
    for gi in range(1, N_GROUPS):
        better = scores[gi] > best_score
        best = jnp.where(better, float(gi), best)
        best_score = jnp.where(better, scores[gi], best_score)

    def pick(rows, slot):
        out = rows[slot]
        for gi in range(1, N_GROUPS):
            out = jnp.where(best == float(gi), rows[gi * g + slot], out)
        return out

    vb = [pick(biased, s) for s in range(g)]
    va = [pick(aff, s) for s in range(g)]

    def argmax_first(vals):
        bi, bv = jnp.zeros_like(vals[0]), vals[0]
        for s in range(1, g):
            better = vals[s] > bv
            bi = jnp.where(better, float(s), bi)
            bv = jnp.where(better, vals[s], bv)
        return bi

    i1 = argmax_first(vb)
    i2 = argmax_first([jnp.where(i1 == float(s), -jnp.inf, vb[s]) for s in range(g)])

    def take(vals, idx):
        out = vals[0]
        for s in range(1, g):
            out = jnp.where(idx == float(s), vals[s], out)
        return out

    a1, a2 = take(va, i1), take(va, i2)
    inv = 1.0 / (a1 + a2)
    return best * g + i1, best * g + i2, a1 * inv, a2 * inv


def _mix_out_kernel(og_ref, om_ref, wo_ref, res_ref, g_ref, b_ref, wr_ref, br_ref,
                    h_ref, hb_ref, comb_ref, route_ref):
    y = _dot(og_ref[...], wo_ref[0:GLA_V_W, :]) + _dot(om_ref[...], wo_ref[GLA_V_W:, :])
    h = _layer_norm(DN_ALPHA * res_ref[...] + y, g_ref[...], b_ref[...])
    h_ref[...] = h
    hb_ref[...] = h.astype(hb_ref.dtype)

    h_hi, h_lo = _split_bf16(h)
    wr = wr_ref[...]
    part_hi = _dot(h_hi, wr).T
    part_lo = _dot(h_lo, wr).T
    e = N_EXPERTS
    logits_t = part_hi[0:e] + part_hi[e:2 * e] + part_lo[0:e]
    e1, e2, w1, w2 = _route_rows([logits_t[x:x + 1, :] for x in range(e)], br_ref)

    tm = h.shape[0]
    sub16 = lax.broadcasted_iota(jnp.int32, (e, tm), 0).astype(jnp.float32)
    comb_t = jnp.where(sub16 == e1, w1, 0.0) + jnp.where(sub16 == e2, w2, 0.0)
    comb_ref[...] = jnp.concatenate([comb_t, jnp.zeros((LANES - e, tm), jnp.float32)], axis=0).T
    sub8 = lax.broadcasted_iota(jnp.int32, (8, tm), 0)
    route_ref[...] = jnp.where(sub8 == 0, e1, jnp.where(sub8 == 1, e2, jnp.where(sub8 == 2, w1, w2)))


def _mix_out(o_gla, o_moba, w_o, h_res, ln_g, ln_b, w_route, b_route, tm=256):
    t = h_res.shape[0]
    row = lambda w: pl.BlockSpec((tm, w), lambda i: (i, 0))
    full = lambda a: pl.BlockSpec(a.shape, lambda i: (0,) * a.ndim)
    return pl.pallas_call(
        _mix_out_kernel,
        grid=(t // tm,),
        in_specs=[row(GLA_V_W), row(MOBA_W), full(w_o), row(D_MODEL), full(ln_g), full(ln_b), full(w_route),
                  pl.BlockSpec(memory_space=pltpu.SMEM)],
        out_specs=[row(D_MODEL), row(D_MODEL), row(LANES), pl.BlockSpec((8, tm), lambda i: (0, i))],
        out_shape=[jax.ShapeDtypeStruct((t, D_MODEL), jnp.float32),
                   jax.ShapeDtypeStruct((t, D_MODEL), jnp.bfloat16),
                   jax.ShapeDtypeStruct((t, LANES), jnp.float32),
                   jax.ShapeDtypeStruct((8, t), jnp.float32)],
        compiler_params=_cparams(("parallel",)),
        name="mix_out_ln_router",
    )(o_gla, o_moba, w_o, h_res, ln_g, ln_b, w_route, b_route)


def _moe_dense_kernel(x_ref, comb_ref, wg_ref, wu_ref, wd_ref, res_ref, g_ref, b_ref, h_ref, hb_ref, acc_ref):
    e = pl.program_id(1)

    @pl.when(e == 0)
    def _():
        acc_ref[...] = jnp.zeros_like(acc_ref)

    x = x_ref[...]
    hidden = (_silu(_dot(x, wg_ref[0])) * _dot(x, wu_ref[0])).astype(jnp.bfloat16)
    y = _dot(hidden, wd_ref[0])
    comb = comb_ref[...]
    lane = lax.broadcasted_iota(jnp.int32, comb.shape, 1)
    weight = jnp.sum(jnp.where(lane == e, comb, 0.0), axis=1, keepdims=True)
    acc_ref[...] += weight * y

    @pl.when(e == N_EXPERTS - 1)
    def _():
        h = _layer_norm(DN_ALPHA * res_ref[...] + acc_ref[...], g_ref[...], b_ref[...])
        h_ref[...] = h
        hb_ref[...] = h.astype(hb_ref.dtype)


def _moe_dense(x_bf16, comb, w_gate, w_up, w_down, h_res, ln_g, ln_b, tm=256):
    t = h_res.shape[0]
    row = lambda w: pl.BlockSpec((tm, w), lambda i, e: (i, 0))
    vec = pl.BlockSpec((1, D_MODEL), lambda i, e: (0, 0))
    return pl.pallas_call(
        _moe_dense_kernel,
        grid=(t // tm, N_EXPERTS),
        in_specs=[row(D_MODEL), row(LANES),
                  pl.BlockSpec((1, D_MODEL, D_EXPERT), lambda i, e: (e, 0, 0)),
                  pl.BlockSpec((1, D_MODEL, D_EXPERT), lambda i, e: (e, 0, 0)),
                  pl.BlockSpec((1, D_EXPERT, D_MODEL), lambda i, e: (e, 0, 0)),
                  row(D_MODEL), vec, vec],
        out_specs=[row(D_MODEL), row(D_MODEL)],
        out_shape=[jax.ShapeDtypeStruct((t, D_MODEL), jnp.float32),
                   jax.ShapeDtypeStruct((t, D_MODEL), jnp.bfloat16)],
        scratch_shapes=[pltpu.VMEM((tm, D_MODEL), jnp.float32)],
        compiler_params=_cparams(("parallel", "arbitrary")),
        name="moe_dense",
    )(x_bf16, comb, w_gate, w_up, w_down, h_res, ln_g, ln_b)


def _rope_tables(positions):
    inv = 1.0 / (ROPE_THETA ** (jnp.arange(0, MOBA_HD, 2, dtype=jnp.float32) / MOBA_HD))
    ang = positions.astype(jnp.float32)[:, None] * inv
    cos, sin = jnp.cos(ang), jnp.sin(ang)
    return jnp.concatenate([cos, cos], axis=-1), jnp.concatenate([-sin, sin], axis=-1)


def _pack_w_in(w_in_l):
    pad = jnp.zeros((D_MODEL, LANES - GLA_GATE_RANK), w_in_l.dtype)
    low = w_in_l[:, LOW_OFF:LOW_OFF + GLA_GATE_RANK]
    return jnp.concatenate([w_in_l[:, :LOW_OFF], w_in_l[:, LOW_OFF + GLA_GATE_RANK:], low, pad],
                           axis=1).astype(jnp.bfloat16)


def kernel(x, positions, w_in, w_gk, b_gk, gla_norm, w_o, ln1_g, ln1_b, w_router, b_router,
           w_gate, w_up, w_down, ln2_g, ln2_b):
    batch, seq, _ = x.shape
    assert batch == 1 and seq % 512 == 0
    cos_full, sin_signed = _rope_tables(positions[0])
    wr_hi, wr_lo = _split_bf16(w_router)
    w_route = jnp.concatenate(
        [wr_hi, wr_lo, jnp.zeros((D_MODEL, LANES - 2 * N_EXPERTS), jnp.bfloat16)], axis=1)

    h = x[0]
    h_bf16 = h.astype(jnp.bfloat16)
    for l in range(DEPTH):
        wgk_pad = jnp.concatenate(
            [w_gk[l], jnp.zeros((LANES - GLA_GATE_RANK, GLA_QK_W), w_gk.dtype)], axis=0).astype(jnp.bfloat16)
        proj = _in_projection(h_bf16, _pack_w_in(w_in[l]))
        o_gla = _gla(proj, wgk_pad, b_gk[l][None, :], gla_norm[l][None, :])
        q_rot, k_rot, kmean = _moba_prep(proj, cos_full, sin_signed)
        o_moba = _moba(q_rot, k_rot, proj, kmean.reshape(MOBA_HEADS, -1, MOBA_HD))
        h, h_bf16, comb, _ = _mix_out(o_gla, o_moba, w_o[l].astype(jnp.bfloat16), h,
                                      ln1_g[l][None, :], ln1_b[l][None, :], w_route, b_router)
        h, h_bf16 = _moe_dense(h_bf16, comb, w_gate[l].astype(jnp.bfloat16), w_up[l].astype(jnp.bfloat16),
                               w_down[l].astype(jnp.bfloat16), h, ln2_g[l][None, :], ln2_b[l][None, :])
    return h[None]
```

```python
import functools
import math

import jax
import jax.numpy as jnp
from jax import lax
from jax.experimental import pallas as pl
from jax.experimental.pallas import tpu as pltpu

D_MODEL = 2048
DEPTH = 2
GLA_HEADS = 4
GLA_DK = 128
GLA_DV = 256
GLA_GATE_RANK = 16
GLA_GATE_NORMALIZER = 16.0
MOBA_HEADS = 8
MOBA_HD = 128
MOBA_BLOCK = 256
MOBA_TOPK = 3
ROPE_THETA = 10000.0
GLA_QK_W = GLA_HEADS * GLA_DK
GLA_V_W = GLA_HEADS * GLA_DV
MOBA_W = MOBA_HEADS * MOBA_HD
N_EXPERTS = 16
N_GROUPS = 4
EXPERTS_PER_GROUP = N_EXPERTS // N_GROUPS
D_EXPERT = 1024
DN_ALPHA = (2.0 * DEPTH) ** 0.25
LN_EPS = 1e-5
RMS_EPS = 1e-6
NEG = -1e30

LANES = 128
GLA_CHUNK = 64
GLA_SUB = 16
VMEM_LIMIT = 48 * 1024 * 1024
ROW_TILES = D_MODEL // LANES
MOE_TM = 256

PROJ_MAIN_W = 2 * GLA_QK_W + 2 * GLA_V_W + 3 * MOBA_W
PROJ_W = PROJ_MAIN_W + LANES
LOW_OFF = 2 * GLA_QK_W + 2 * GLA_V_W
PROJ_TN = 896
COL_GQ = 0
COL_GK = GLA_QK_W // GLA_DK
COL_GV = (2 * GLA_QK_W) // GLA_DV
COL_GG = (2 * GLA_QK_W + GLA_V_W) // GLA_DV
COL_MQ = LOW_OFF // MOBA_HD
COL_MK = COL_MQ + MOBA_HEADS
COL_MV = COL_MK + MOBA_HEADS
COL_LOW = PROJ_MAIN_W // LANES


def _cparams(semantics):
    return pltpu.CompilerParams(dimension_semantics=semantics, vmem_limit_bytes=VMEM_LIMIT)


def _split_bf16(x):
    hi = x.astype(jnp.bfloat16)
    lo = (x - hi.astype(jnp.float32)).astype(jnp.bfloat16)
    return hi, lo


def _dot(a, b):
    return jnp.dot(a, b, preferred_element_type=jnp.float32)


def _dot_nt(a, b):
    return lax.dot_general(a, b, (((1,), (1,)), ((), ())), preferred_element_type=jnp.float32)


def _layer_norm(y, g, b):
    mu = jnp.mean(y, axis=-1, keepdims=True)
    yc = y - mu
    var = jnp.mean(yc * yc, axis=-1, keepdims=True)
    return yc * lax.rsqrt(var + LN_EPS) * g + b


def _silu(x):
    return x * (1.0 / (1.0 + jnp.exp(-x)))


def _proj_kernel(x_ref, w_ref, o_ref):
    o_ref[...] = _dot(x_ref[...], w_ref[...]).astype(o_ref.dtype)


def _in_projection(h_bf16, w_packed, tm=512):
    t = h_bf16.shape[0]
    return pl.pallas_call(
        _proj_kernel,
        grid=(PROJ_W // PROJ_TN, t // tm),
        in_specs=[pl.BlockSpec((tm, D_MODEL), lambda j, i: (i, 0)),
                  pl.BlockSpec((D_MODEL, PROJ_TN), lambda j, i: (0, j))],
        out_specs=pl.BlockSpec((tm, PROJ_TN), lambda j, i: (i, j)),
        out_shape=jax.ShapeDtypeStruct((t, PROJ_W), jnp.bfloat16),
        compiler_params=_cparams(("parallel", "parallel")),
        name="in_projection",
    )(h_bf16, w_packed)


def _gla_kernel(q_ref, k_ref, v_ref, gg_ref, low_ref, wgk_ref, bgk_ref, norm_ref, o_ref, st_ref, *, n_chunks):
    c_len, sub = GLA_CHUNK, GLA_SUB
    n_sub = c_len // sub

    @pl.when(pl.program_id(1) == 0)
    def _():
        st_ref[...] = jnp.zeros_like(st_ref)

    row = lax.broadcasted_iota(jnp.int32, (c_len, c_len), 0)
    col = lax.broadcasted_iota(jnp.int32, (c_len, c_len), 1)
    tril = jnp.where(col <= row, 1.0, 0.0).astype(jnp.bfloat16)
    sub_row = lax.broadcasted_iota(jnp.int32, (sub, GLA_DK), 0)
    lane_c = lax.broadcasted_iota(jnp.int32, (sub, c_len), 1)
    wgk = wgk_ref[...]
    bgk = bgk_ref[...]
    gnorm = norm_ref[...]

    def chunk(c, carry):
        r0 = pl.multiple_of(c * c_len, c_len)
        q = q_ref[pl.ds(r0, c_len), :].astype(jnp.float32) * (GLA_DK ** -0.5)
        k = k_ref[pl.ds(r0, c_len), :].astype(jnp.float32)
        v = v_ref[pl.ds(r0, c_len), :]
        x = _dot(low_ref[pl.ds(r0, c_len), :], wgk) + bgk
        g = (jnp.minimum(x, 0.0) - jnp.log1p(jnp.exp(-jnp.abs(x)))) * (1.0 / GLA_GATE_NORMALIZER)
        g_hi, g_lo = _split_bf16(g)
        b = _dot(tril, g_hi) + _dot(tril, g_lo)
        b_last = b[c_len - 1:c_len, :]

        st = st_ref[...]
        inter = _dot_nt((q * jnp.exp(b)).astype(jnp.bfloat16), st.astype(jnp.bfloat16))
        k_dec = (k * jnp.exp(b_last - b)).astype(jnp.bfloat16)
        st_ref[...] = st * jnp.exp(b_last) + _dot(v.T, k_dec)

        blocks = []
        for i_sub in range(n_sub):
            lo = i_sub * sub
            q_i, k_i, b_i = q[lo:lo + sub], k[lo:lo + sub], b[lo:lo + sub]
            a_i = jnp.zeros((sub, c_len), jnp.float32)
            for j in range(sub):
                e = jnp.exp(jnp.minimum(b_i - b_i[j:j + 1, :], 0.0))
                term = jnp.where(sub_row >= j, q_i * e * k_i[j:j + 1, :], 0.0)
                a_i = jnp.where(lane_c == lo + j, jnp.sum(term, axis=1, keepdims=True), a_i)
            if i_sub > 0:
                b_start = b[lo - 1:lo, :]
                q_s = (q_i * jnp.exp(b_i - b_start)).astype(jnp.bfloat16)
                k_s = (k * jnp.exp(jnp.minimum(b_start - b, 0.0))).astype(jnp.bfloat16)
                a_i = jnp.where(lane_c < lo, _dot_nt(q_s, k_s), a_i)
            blocks.append(a_i)
        attn = jnp.concatenate(blocks, axis=0).astype(jnp.bfloat16)
        o = inter + _dot(attn, v)

        o = o * lax.rsqrt(jnp.mean(o * o, axis=-1, keepdims=True) + RMS_EPS)
        gate = gg_ref[pl.ds(r0, c_len), :].astype(jnp.float32)
        o_ref[pl.ds(r0, c_len), :] = (o * gnorm * _silu(gate)).astype(o_ref.dtype)
        return carry

    lax.fori_loop(0, n_chunks, chunk, 0)


def _gla(proj, wgk_pad, b_gk, gla_norm, bt=512):
    t = proj.shape[0]
    kern = functools.partial(_gla_kernel, n_chunks=bt // GLA_CHUNK)
    return pl.pallas_call(
        kern,
        grid=(GLA_HEADS, t // bt),
        in_specs=[pl.BlockSpec((bt, GLA_DK), lambda h, i: (i, COL_GQ + h)),
                  pl.BlockSpec((bt, GLA_DK), lambda h, i: (i, COL_GK + h)),
                  pl.BlockSpec((bt, GLA_DV), lambda h, i: (i, COL_GV + h)),
                  pl.BlockSpec((bt, GLA_DV), lambda h, i: (i, COL_GG + h)),
                  pl.BlockSpec((bt, LANES), lambda h, i: (i, COL_LOW)),
                  pl.BlockSpec((LANES, GLA_DK), lambda h, i: (0, h)),
                  pl.BlockSpec((1, GLA_DK), lambda h, i: (0, h)),
                  pl.BlockSpec((1, GLA_DV), lambda h, i: (0, 0))],
        out_specs=pl.BlockSpec((bt, GLA_DV), lambda h, i: (i, h)),
        out_shape=jax.ShapeDtypeStruct((t, GLA_V_W), jnp.bfloat16),
        scratch_shapes=[pltpu.VMEM((GLA_DV, GLA_DK), jnp.float32)],
        compiler_params=_cparams(("parallel", "arbitrary")),
        name="gla",
    )(proj, proj, proj, proj, proj, wgk_pad, b_gk, gla_norm)


def _rope(x, cos_full, sin_signed):
    return x * cos_full + pltpu.roll(x, MOBA_HD // 2, axis=1) * sin_signed


def _moba_prep_kernel(q_ref, k_ref, cos_ref, sin_ref, qo_ref, ko_ref, km_ref):
    cos_full, sin_signed = cos_ref[...], sin_ref[...]
    q = _rope(q_ref[...].astype(jnp.float32), cos_full, sin_signed)
    k = _rope(k_ref[...].astype(jnp.float32), cos_full, sin_signed)
    qo_ref[0] = (q * (MOBA_HD ** -0.5)).astype(qo_ref.dtype)
    ko_ref[0] = k.astype(ko_ref.dtype)
    km_ref[0, 0] = jnp.mean(k, axis=0, keepdims=True)


def _moba_prep(proj, cos_full, sin_signed):
    t = proj.shape[0]
    nb = t // MOBA_BLOCK
    blk = pl.BlockSpec((1, MOBA_BLOCK, MOBA_HD), lambda h, i: (h, i, 0))
    return pl.pallas_call(
        _moba_prep_kernel,
        grid=(MOBA_HEADS, nb),
        in_specs=[pl.BlockSpec((MOBA_BLOCK, MOBA_HD), lambda h, i: (i, COL_MQ + h)),
                  pl.BlockSpec((MOBA_BLOCK, MOBA_HD), lambda h, i: (i, COL_MK + h)),
                  pl.BlockSpec((MOBA_BLOCK, MOBA_HD), lambda h, i: (i, 0)),
                  pl.BlockSpec((MOBA_BLOCK, MOBA_HD), lambda h, i: (i, 0))],
        out_specs=[blk, blk, pl.BlockSpec((1, 1, 1, MOBA_HD), lambda h, i: (h, i, 0, 0))],
        out_shape=[jax.ShapeDtypeStruct((MOBA_HEADS, t, MOBA_HD), jnp.bfloat16),
                   jax.ShapeDtypeStruct((MOBA_HEADS, t, MOBA_HD), jnp.bfloat16),
                   jax.ShapeDtypeStruct((MOBA_HEADS, nb, 1, MOBA_HD), jnp.float32)],
        compiler_params=_cparams(("parallel", "parallel")),
        name="moba_prep",
    )(proj, proj, cos_full, sin_signed)


def _moba_kernel(q_ref, k_ref, v_ref, km_ref, o_ref, *, nb):
    blk = MOBA_BLOCK
    i = pl.program_id(1)
    q = q_ref[0]

    km_hi, km_lo = _split_bf16(km_ref[0])
    gate = _dot_nt(q, km_hi) + _dot_nt(q, km_lo)
    lane = lax.broadcasted_iota(jnp.int32, (blk, nb), 1).astype(jnp.float32)
    gate = jnp.where(lane < i.astype(jnp.float32), gate, NEG)
    sels = []
    for _ in range(MOBA_TOPK):
        m = jnp.max(gate, axis=1, keepdims=True)
        idx = jnp.min(jnp.where(gate == m, lane, float(nb)), axis=1, keepdims=True)
        sels.append(jnp.where(m > 0.5 * NEG, idx, -1.0))
        gate = jnp.where(lane == idx, NEG, gate)

    r0 = pl.multiple_of(i * blk, blk)
    s = _dot_nt(q, k_ref[0, pl.ds(r0, blk), :])
    row = lax.broadcasted_iota(jnp.int32, (blk, blk), 0)
    col = lax.broadcasted_iota(jnp.int32, (blk, blk), 1)
    s = jnp.where(col <= row, s, NEG)
    m0 = jnp.max(s, axis=1, keepdims=True)
    p = jnp.exp(s - m0)
    l0 = jnp.sum(p, axis=1, keepdims=True)
    acc0 = _dot(p.astype(jnp.bfloat16), v_ref[pl.ds(r0, blk), :])

    def past_block(j, carry):
        m_run, l_run, acc = carry
        jf = j.astype(jnp.float32)
        picked = (sels[0] == jf) | (sels[1] == jf) | (sels[2] == jf)
        bias = jnp.where(picked, 0.0, NEG)
        c0 = pl.multiple_of(j * blk, blk)
        s_j = _dot_nt(q, k_ref[0, pl.ds(c0, blk), :]) + bias
        m_new = jnp.maximum(m_run, jnp.max(s_j, axis=1, keepdims=True))
        alpha = jnp.exp(m_run - m_new)
        p_j = jnp.exp(s_j - m_new)
        l_new = alpha * l_run + jnp.sum(p_j, axis=1, keepdims=True)
        acc_new = alpha * acc + _dot(p_j.astype(jnp.bfloat16), v_ref[pl.ds(c0, blk), :])
        return m_new, l_new, acc_new

    _, l_fin, acc_fin = lax.fori_loop(0, i, past_block, (m0, l0, acc0))
    o_ref[...] = (acc_fin / l_fin).astype(o_ref.dtype)


def _moba(q_rot, k_rot, proj, kmean):
    t = proj.shape[0]
    nb = t // MOBA_BLOCK
    return pl.pallas_call(
        functools.partial(_moba_kernel, nb=nb),
        grid=(MOBA_HEADS, nb),
        in_specs=[pl.BlockSpec((1, MOBA_BLOCK, MOBA_HD), lambda h, i: (h, i, 0)),
                  pl.BlockSpec((1, t, MOBA_HD), lambda h, i: (h, 0, 0)),
                  pl.BlockSpec((t, MOBA_HD), lambda h, i: (0, COL_MV + h)),
                  pl.BlockSpec((1, nb, MOBA_HD), lambda h, i: (h, 0, 0))],
        out_specs=pl.BlockSpec((MOBA_BLOCK, MOBA_HD), lambda h, i: (i, h)),
        out_shape=jax.ShapeDtypeStruct((t, MOBA_W), jnp.bfloat16),
        compiler_params=_cparams(("parallel", "arbitrary")),
        name="moba_attention",
    )(q_rot, k_rot, proj, kmean)


def _route_rows(logit_rows, bias_ref):
    mx = functools.reduce(jnp.maximum, logit_rows)
    ex = [jnp.exp(l - mx) for l in logit_rows]
    inv_den = 1.0 / functools.reduce(lambda a, b: a + b, ex)
    aff = [e * inv_den for e in ex]
    biased = [a + bias_ref[e] for e, a in enumerate(aff)]

    def top2_sum(a0, a1, a2, a3):
        hi01, lo01 = jnp.maximum(a0, a1), jnp.minimum(a0, a1)
        hi23, lo23 = jnp.maximum(a2, a3), jnp.minimum(a2, a3)
        return jnp.maximum(hi01, hi23) + jnp.maximum(jnp.minimum(hi01, hi23), jnp.maximum(lo01, lo23))

    g = EXPERTS_PER_GROUP
    scores = [top2_sum(*biased[gi * g:(gi + 1) * g]) for gi in range(N_GROUPS)]
    best, best_score = jnp.zeros_like(scores[0]), scores[0]
    for gi in range(1, N_GROUPS):
        better = scores[gi] > best_score
        best = jnp.where(better, float(gi), best)
        best_score = jnp.where(better, scores[gi], best_score)

    def pick(rows, slot):
        out = rows[slot]
        for gi in range(1, N_GROUPS):
            out = jnp.where(best == float(gi), rows[gi * g + slot], out)
        return out

    vb = [pick(biased, s) for s in range(g)]
    va = [pick(aff, s) for s in range(g)]

    def argmax_first(vals):
        bi, bv = jnp.zeros_like(vals[0]), vals[0]
        for s in range(1, g):
            better = vals[s] > bv
            bi = jnp.where(better, float(s), bi)
            bv = jnp.where(better, vals[s], bv)
        return bi

    i1 = argmax_first(vb)
    i2 = argmax_first([jnp.where(i1 == float(s), -jnp.inf, vb[s]) for s in range(g)])

    def take(vals, idx):
        out = vals[0]
        for s in range(1, g):
            out = jnp.where(idx == float(s), vals[s], out)
        return out

    a1, a2 = take(va, i1), take(va, i2)
    inv = 1.0 / (a1 + a2)
    return best * g + i1, best * g + i2, a1 * inv, a2 * inv


def _mix_out_kernel(og_ref, om_ref, wo_ref, res_ref, g_ref, b_ref, wr_ref, br_ref,
                    h_ref, hb3_ref, route_ref, route_t_ref):
    y = _dot(og_ref[...], wo_ref[0:GLA_V_W, :]) + _dot(om_ref[...], wo_ref[GLA_V_W:, :])
    h = _layer_norm(DN_ALPHA * res_ref[...] + y, g_ref[...], b_ref[...])
    h_ref[...] = h
    tm = h.shape[0]
    hb3_ref[...] = h.astype(hb3_ref.dtype).reshape(tm, ROW_TILES, LANES)

    h_hi, h_lo = _split_bf16(h)
    wr = wr_ref[...]
    part_hi = _dot(h_hi, wr).T
    part_lo = _dot(h_lo, wr).T
    e = N_EXPERTS
    logits_t = part_hi[0:e] + part_hi[e:2 * e] + part_lo[0:e]
    e1, e2, w1, w2 = _route_rows([logits_t[x:x + 1, :] for x in range(e)], br_ref)

    sub8 = lax.broadcasted_iota(jnp.int32, (8, tm), 0)
    route = jnp.where(sub8 == 0, e1, jnp.where(sub8 == 1, e2, jnp.where(sub8 == 2, w1, w2)))
    route_ref[...] = route
    route_t_ref[...] = jnp.concatenate([route, jnp.zeros((LANES - 8, tm), jnp.float32)], axis=0).T


def _mix_out(o_gla, o_moba, w_o, h_res, ln_g, ln_b, w_route, b_route, tm=256):
    t = h_res.shape[0]
    row = lambda w: pl.BlockSpec((tm, w), lambda i: (i, 0))
    full = lambda a: pl.BlockSpec(a.shape, lambda i: (0,) * a.ndim)
    return pl.pallas_call(
        _mix_out_kernel,
        grid=(t // tm,),
        in_specs=[row(GLA_V_W), row(MOBA_W), full(w_o), row(D_MODEL), full(ln_g), full(ln_b), full(w_route),
                  pl.BlockSpec(memory_space=pltpu.SMEM)],
        out_specs=[row(D_MODEL), pl.BlockSpec((tm, ROW_TILES, LANES), lambda i: (i, 0, 0)),
                   pl.BlockSpec((8, tm), lambda i: (0, i)), row(LANES)],
        out_shape=[jax.ShapeDtypeStruct((t, D_MODEL), jnp.float32),
                   jax.ShapeDtypeStruct((t, ROW_TILES, LANES), jnp.bfloat16),
                   jax.ShapeDtypeStruct((8, t), jnp.float32),
                   jax.ShapeDtypeStruct((t, LANES), jnp.float32)],
        compiler_params=_cparams(("parallel",)),
        name="mix_out_ln_router",
    )(o_gla, o_moba, w_o, h_res, ln_g, ln_b, w_route, b_route)


def _moe_plan_kernel(route_ref, plan_ref, cnt_ref, carry_ref):
    @pl.when(pl.program_id(0) == 0)
    def _():
        carry_ref[...] = jnp.zeros_like(carry_ref)

    tm = route_ref.shape[1]
    e1, e2 = route_ref[0:1, :], route_ref[1:2, :]
    sub = lax.broadcasted_iota(jnp.int32, (N_EXPERTS, tm), 0).astype(jnp.float32)
    is1, is2 = sub == e1, sub == e2
    member = jnp.where(is1, 1.0, 0.0) + jnp.where(is2, 1.0, 0.0)
    row = lax.broadcasted_iota(jnp.int32, (tm, tm), 0)
    col = lax.broadcasted_iota(jnp.int32, (tm, tm), 1)
    earlier = jnp.where(row < col, 1.0, 0.0).astype(jnp.bfloat16)
    prefix = _dot(member.astype(jnp.bfloat16), earlier) + carry_ref[:, 0:1]
    r1 = jnp.sum(jnp.where(is1, prefix, 0.0), axis=0, keepdims=True)
    r2 = jnp.sum(jnp.where(is2, prefix, 0.0), axis=0, keepdims=True)
    carry_ref[...] += jnp.sum(member, axis=1, keepdims=True)
    sub8 = lax.broadcasted_iota(jnp.int32, (8, tm), 0)
    plan = jnp.where(sub8 == 0, e1, jnp.where(sub8 == 1, e2, jnp.where(sub8 == 2, r1, r2)))
    plan_ref[...] = plan.astype(jnp.int32)
    cnt_ref[...] = carry_ref[...]


def _moe_plan(route, tm=512):
    t = route.shape[1]
    return pl.pallas_call(
        _moe_plan_kernel,
        grid=(t // tm,),
        in_specs=[pl.BlockSpec((8, tm), lambda i: (0, i))],
        out_specs=[pl.BlockSpec((8, tm), lambda i: (0, i)), pl.BlockSpec((N_EXPERTS, LANES), lambda i: (0, 0))],
        out_shape=[jax.ShapeDtypeStruct((8, t), jnp.int32),
                   jax.ShapeDtypeStruct((N_EXPERTS, LANES), jnp.float32)],
        scratch_shapes=[pltpu.VMEM((N_EXPERTS, LANES), jnp.float32)],
        compiler_params=_cparams(("arbitrary",)),
        name="moe_plan",
    )(route)


def _sorted_rows(offs_ref, plan_ref, t):
    return (offs_ref[plan_ref[0, t]] + plan_ref[2, t], offs_ref[plan_ref[1, t]] + plan_ref[3, t])


def _moe_dispatch_kernel(offs_ref, plan_ref, h3_ref, xs_in_ref, xs_ref, sem):
    del xs_in_ref
    tm = plan_ref.shape[1]
    base = pl.program_id(0) * tm

    def copies(t):
        src = h3_ref.at[base + t]
        return [pltpu.make_async_copy(src, xs_ref.at[p], sem) for p in _sorted_rows(offs_ref, plan_ref, t)]

    def start(t, c):
        for cp in copies(t):
            cp.start()
        return c

    def wait(t, c):
        for cp in copies(t):
            cp.wait()
        return c

    lax.fori_loop(0, tm, start, 0)
    lax.fori_loop(0, tm, wait, 0)


def _moe_dispatch(offs, plan, h3, n_rows, tm=512):
    t = h3.shape[0]
    xs0 = jnp.zeros((n_rows, ROW_TILES, LANES), h3.dtype)
    any_spec = pl.BlockSpec(memory_space=pl.ANY)
    return pl.pallas_call(
        _moe_dispatch_kernel,
        grid_spec=pltpu.PrefetchScalarGridSpec(
            num_scalar_prefetch=1,
            grid=(t // tm,),
            in_specs=[pl.BlockSpec((8, tm), lambda i, offs: (0, i), memory_space=pltpu.SMEM), any_spec, any_spec],
            out_specs=any_spec,
            scratch_shapes=[pltpu.SemaphoreType.DMA(())]),
        out_shape=jax.ShapeDtypeStruct(xs0.shape, xs0.dtype),
        input_output_aliases={3: 0},
        compiler_params=_cparams(("arbitrary",)),
        name="moe_dispatch",
    )(offs, plan, h3, xs0)


def _moe_group_kernel(tile_expert_ref, n_active_ref, x3_ref, wg_ref, wu_ref, wd_ref, o3_ref):
    del tile_expert_ref
    r = pl.program_id(0)
    tm = x3_ref.shape[0]

    @pl.when(r < n_active_ref[0])
    def _():
        x = x3_ref[...].reshape(tm, D_MODEL)
        hidden = (_silu(_dot(x, wg_ref[0])) * _dot(x, wu_ref[0])).astype(jnp.bfloat16)
        o3_ref[...] = _dot(hidden, wd_ref[0]).astype(o3_ref.dtype).reshape(tm, ROW_TILES, LANES)

    @pl.when(r >= n_active_ref[0])
    def _():
        o3_ref[...] = jnp.zeros_like(o3_ref)


def _moe_group(tile_expert, n_active, xs3, w_gate, w_up, w_down):
    n_rows = xs3.shape[0]
    tile = pl.BlockSpec((MOE_TM, ROW_TILES, LANES), lambda r, te, na: (r, 0, 0))
    return pl.pallas_call(
        _moe_group_kernel,
        grid_spec=pltpu.PrefetchScalarGridSpec(
            num_scalar_prefetch=2,
            grid=(n_rows // MOE_TM,),
            in_specs=[tile,
                      pl.BlockSpec((1, D_MODEL, D_EXPERT), lambda r, te, na: (te[r], 0, 0)),
                      pl.BlockSpec((1, D_MODEL, D_EXPERT), lambda r, te, na: (te[r], 0, 0)),
                      pl.BlockSpec((1, D_EXPERT, D_MODEL), lambda r, te, na: (te[r], 0, 0))],
            out_specs=tile),
        out_shape=jax.ShapeDtypeStruct(xs3.shape, jnp.bfloat16),
        compiler_params=_cparams(("arbitrary",)),
        name="moe_grouped_swiglu",
    )(tile_expert, n_active, xs3, w_gate, w_up, w_down)


def _moe_combine_kernel(offs_ref, plan_ref, o3_ref, route_t_ref, res_ref, g_ref, b_ref, h_ref, hb_ref,
                        buf1, buf2, sem):
    tm = plan_ref.shape[1]

    def copies(t):
        p1, p2 = _sorted_rows(offs_ref, plan_ref, t)
        return [pltpu.make_async_copy(o3_ref.at[p1], buf1.at[t], sem),
                pltpu.make_async_copy(o3_ref.at[p2], buf2.at[t], sem)]

    def start(t, c):
        for cp in copies(t):
            cp.start()
        return c

    def wait(t, c):
        for cp in copies(t):
            cp.wait()
        return c

    lax.fori_loop(0, tm, start, 0)
    lax.fori_loop(0, tm, wait, 0)
    route_t = route_t_ref[...]
    y = (route_t[:, 2:3] * buf1[...].reshape(tm, D_MODEL).astype(jnp.float32)
         + route_t[:, 3:4] * buf2[...].reshape(tm, D_MODEL).astype(jnp.float32))
    h = _layer_norm(DN_ALPHA * res_ref[...] + y, g_ref[...], b_ref[...])
    h_ref[...] = h
    hb_ref[...] = h.astype(hb_ref.dtype)


def _moe_combine(offs, plan, o3, route_t, h_res, ln_g, ln_b, tm=256):
    t = h_res.shape[0]
    row = lambda w: pl.BlockSpec((tm, w), lambda i, offs: (i, 0))
    vec = pl.BlockSpec((1, D_MODEL), lambda i, offs: (0, 0))
    return pl.pallas_call(
        _moe_combine_kernel,
        grid_spec=pltpu.PrefetchScalarGridSpec(
            num_scalar_prefetch=1,
            grid=(t // tm,),
            in_specs=[pl.BlockSpec((8, tm), lambda i, offs: (0, i), memory_space=pltpu.SMEM),
                      pl.BlockSpec(memory_space=pl.ANY), row(LANES), row(D_MODEL), vec, vec],
            out_specs=[row(D_MODEL), row(D_MODEL)],
            scratch_shapes=[pltpu.VMEM((tm, ROW_TILES, LANES), jnp.bfloat16),
                            pltpu.VMEM((tm, ROW_TILES, LANES), jnp.bfloat16),
                            pltpu.SemaphoreType.DMA(())]),
        out_shape=[jax.ShapeDtypeStruct((t, D_MODEL), jnp.float32),
                   jax.ShapeDtypeStruct((t, D_MODEL), jnp.bfloat16)],
        compiler_params=_cparams(("arbitrary",)),
        name="moe_combine_ln",
    )(offs, plan, o3, route_t, h_res, ln_g, ln_b)


def _moe(h, hb3, route, route_t, w_gate, w_up, w_down, ln_g, ln_b):
    t = h.shape[0]
    plan, cnt = _moe_plan(route)
    counts = cnt[:, 0].astype(jnp.int32)
    tiles_per_expert = (counts + MOE_TM - 1) // MOE_TM
    tile_end = jnp.cumsum(tiles_per_expert)
    offs = ((tile_end - tiles_per_expert) * MOE_TM).astype(jnp.int32)
    n_tiles = 2 * t // MOE_TM + N_EXPERTS
    tile_expert = jnp.minimum(
        jnp.sum(jnp.arange(n_tiles, dtype=jnp.int32)[:, None] >= tile_end[None, :], axis=1), N_EXPERTS - 1
    ).astype(jnp.int32)
    xs3 = _moe_dispatch(offs, plan, hb3, n_tiles * MOE_TM)
    o3 = _moe_group(tile_expert, tile_end[-1:].astype(jnp.int32), xs3, w_gate, w_up, w_down)
    return _moe_combine(offs, plan, o3, route_t, h, ln_g, ln_b)


def _rope_tables(positions):
    inv = 1.0 / (ROPE_THETA ** (jnp.arange(0, MOBA_HD, 2, dtype=jnp.float32) / MOBA_HD))
    ang = positions.astype(jnp.float32)[:, None] * inv
    cos, sin = jnp.cos(ang), jnp.sin(ang)
    return jnp.concatenate([cos, cos], axis=-1), jnp.concatenate([-sin, sin], axis=-1)


def _pack_w_in(w_in_l):
    pad = jnp.zeros((D_MODEL, LANES - GLA_GATE_RANK), w_in_l.dtype)
    low = w_in_l[:, LOW_OFF:LOW_OFF + GLA_GATE_RANK]
    return jnp.concatenate([w_in_l[:, :LOW_OFF], w_in_l[:, LOW_OFF + GLA_GATE_RANK:], low, pad],
                           axis=1).astype(jnp.bfloat16)


def kernel(x, positions, w_in, w_gk, b_gk, gla_norm, w_o, ln1_g, ln1_b, w_router, b_router,
           w_gate, w_up, w_down, ln2_g, ln2_b):
    batch, seq, _ = x.shape
    assert batch == 1 and seq % 512 == 0
    cos_full, sin_signed = _rope_tables(positions[0])
    wr_hi, wr_lo = _split_bf16(w_router)
    w_route = jnp.concatenate(
        [wr_hi, wr_lo, jnp.zeros((D_MODEL, LANES - 2 * N_EXPERTS), jnp.bfloat16)], axis=1)

    h = x[0]
    h_bf16 = h.astype(jnp.bfloat16)
    for l in range(DEPTH):
        wgk_pad = jnp.concatenate(
            [w_gk[l], jnp.zeros((LANES - GLA_GATE_RANK, GLA_QK_W), w_gk.dtype)], axis=0).astype(jnp.bfloat16)
        proj = _in_projection(h_bf16, _pack_w_in(w_in[l]))
        o_gla = _gla(proj, wgk_pad, b_gk[l][None, :], gla_norm[l][None, :])
        q_rot, k_rot, kmean = _moba_prep(proj, cos_full, sin_signed)
        o_moba = _moba(q_rot, k_rot, proj, kmean.reshape(MOBA_HEADS, -1, MOBA_HD))
        h, hb3, route, route_t = _mix_out(o_gla, o_moba, w_o[l].astype(jnp.bfloat16), h,
                                          ln1_g[l][None, :], ln1_b[l][None, :], w_route, b_router)
        h, h_bf16 = _moe(h, hb3, route, route_t, w_gate[l].astype(jnp.bfloat16), w_up[l].astype(jnp.bfloat16),
                         w_down[l].astype(jnp.bfloat16), ln2_g[l][None, :], ln2_b[l][None, :])
    return h[None]
```

```python
import functools
import math

import jax
import jax.numpy as jnp
from jax import lax
from jax.experimental import pallas as pl
from jax.experimental.pallas import tpu as pltpu

D_MODEL = 2048
DEPTH = 2
GLA_HEADS = 4
GLA_DK = 128
GLA_DV = 256
GLA_GATE_RANK = 16
GLA_GATE_NORMALIZER = 16.0
MOBA_HEADS = 8
MOBA_HD = 128
MOBA_BLOCK = 256
MOBA_TOPK = 3
MOBA_BLOCK_SHIFT = 8
MOBA_GROUP = 4
MOBA_VT_ROWS = MOBA_HD + 16
LOG2_E = 1.4426950408889634
ROPE_THETA = 10000.0
GLA_QK_W = GLA_HEADS * GLA_DK
GLA_V_W = GLA_HEADS * GLA_DV
MOBA_W = MOBA_HEADS * MOBA_HD
N_EXPERTS = 16
N_GROUPS = 4
EXPERTS_PER_GROUP = N_EXPERTS // N_GROUPS
D_EXPERT = 1024
DN_ALPHA = (2.0 * DEPTH) ** 0.25
LN_EPS = 1e-5
RMS_EPS = 1e-6
NEG = -1e30

LANES = 128
GLA_CHUNK = 64
GLA_SUB = 16
VMEM_LIMIT = 48 * 1024 * 1024
ROW_TILES = D_MODEL // LANES
MOE_TM = 256

PROJ_MAIN_W = 2 * GLA_QK_W + 2 * GLA_V_W + 3 * MOBA_W
PROJ_W = PROJ_MAIN_W + LANES
LOW_OFF = 2 * GLA_QK_W + 2 * GLA_V_W
PROJ_TN = 896
COL_GQ, COL_GK = 0, 1
COL_GV = (2 * GLA_QK_W) // GLA_V_W
COL_GG = (2 * GLA_QK_W + GLA_V_W) // GLA_V_W
COL_MQ = LOW_OFF // MOBA_HD
COL_MK = COL_MQ + MOBA_HEADS
COL_MV = COL_MK + MOBA_HEADS
COL_LOW = PROJ_MAIN_W // LANES


def _cparams(semantics):
    return pltpu.CompilerParams(dimension_semantics=semantics, vmem_limit_bytes=VMEM_LIMIT)


def _split_bf16(x):
    hi = x.astype(jnp.bfloat16)
    lo = (x - hi.astype(jnp.float32)).astype(jnp.bfloat16)
    return hi, lo


def _dot(a, b):
    return jnp.dot(a, b, preferred_element_type=jnp.float32)


def _dot_nt(a, b):
    return lax.dot_general(a, b, (((1,), (1,)), ((), ())), preferred_element_type=jnp.float32)


def _layer_norm(y, g, b):
    mu = jnp.mean(y, axis=-1, keepdims=True)
    yc = y - mu
    var = jnp.mean(yc * yc, axis=-1, keepdims=True)
    return yc * lax.rsqrt(var + LN_EPS) * g + b


def _silu(x):
    return x * (1.0 / (1.0 + jnp.exp(-x)))


def _proj_kernel(x_ref, w_ref, o_ref):
    o_ref[...] = _dot(x_ref[...], w_ref[...]).astype(o_ref.dtype)


def _in_projection(h_bf16, w_packed, tm=512):
    t = h_bf16.shape[0]
    return pl.pallas_call(
        _proj_kernel,
        grid=(PROJ_W // PROJ_TN, t // tm),
        in_specs=[pl.BlockSpec((tm, D_MODEL), lambda j, i: (i, 0)),
                  pl.BlockSpec((D_MODEL, PROJ_TN), lambda j, i: (0, j))],
        out_specs=pl.BlockSpec((tm, PROJ_TN), lambda j, i: (i, j)),
        out_shape=jax.ShapeDtypeStruct((t, PROJ_W), jnp.bfloat16),
        compiler_params=_cparams(("parallel", "parallel")),
        name="in_projection",
    )(h_bf16, w_packed)


def _gla_head_chunk(q, k, v, x, gate, gnorm, st_ref, consts):
    tril, sub_row, lane_c = consts
    c_len, sub = GLA_CHUNK, GLA_SUB
    g = (jnp.minimum(x, 0.0) - jnp.log1p(jnp.exp(-jnp.abs(x)))) * (1.0 / GLA_GATE_NORMALIZER)
    g_hi, g_lo = _split_bf16(g)
    b = _dot(tril, g_hi) + _dot(tril, g_lo)
    b_last = b[c_len - 1:c_len, :]

    st = st_ref[...]
    inter = _dot_nt((q * jnp.exp(b)).astype(jnp.bfloat16), st.astype(jnp.bfloat16))
    k_dec = (k * jnp.exp(b_last - b)).astype(jnp.bfloat16)
    st_ref[...] = st * jnp.exp(b_last) + _dot(v.T, k_dec)

    blocks = []
    for i_sub in range(c_len // sub):
        lo = i_sub * sub
        q_i, k_i, b_i = q[lo:lo + sub], k[lo:lo + sub], b[lo:lo + sub]
        a_i = jnp.zeros((sub, c_len), jnp.float32)
        for j in range(sub):
            e = jnp.exp(jnp.minimum(b_i - b_i[j:j + 1, :], 0.0))
            term = jnp.where(sub_row >= j, q_i * e * k_i[j:j + 1, :], 0.0)
            a_i = jnp.where(lane_c == lo + j, jnp.sum(term, axis=1, keepdims=True), a_i)
        if i_sub > 0:
            b_start = b[lo - 1:lo, :]
            q_s = (q_i * jnp.exp(b_i - b_start)).astype(jnp.bfloat16)
            k_s = (k * jnp.exp(jnp.minimum(b_start - b, 0.0))).astype(jnp.bfloat16)
            a_i = jnp.where(lane_c < lo, _dot_nt(q_s, k_s), a_i)
        blocks.append(a_i)
    attn = jnp.concatenate(blocks, axis=0).astype(jnp.bfloat16)
    o = inter + _dot(attn, v)
    o = o * lax.rsqrt(jnp.mean(o * o, axis=-1, keepdims=True) + RMS_EPS)
    return o * gnorm * _silu(gate)


def _gla_kernel(q_ref, k_ref, v_ref, gg_ref, low_ref, wgk_ref, bgk_ref, norm_ref, o_ref, st_ref, *, n_chunks):
    c_len, sub = GLA_CHUNK, GLA_SUB

    @pl.when(pl.program_id(0) == 0)
    def _():
        st_ref[...] = jnp.zeros_like(st_ref)

    row = lax.broadcasted_iota(jnp.int32, (c_len, c_len), 0)
    col = lax.broadcasted_iota(jnp.int32, (c_len, c_len), 1)
    consts = (jnp.where(col <= row, 1.0, 0.0).astype(jnp.bfloat16),
              lax.broadcasted_iota(jnp.int32, (sub, GLA_DK), 0),
              lax.broadcasted_iota(jnp.int32, (sub, c_len), 1))
    gnorm = norm_ref[...]

    def chunk(c, carry):
        rows = pl.ds(pl.multiple_of(c * c_len, c_len), c_len)
        x_all = _dot(low_ref[rows, :], wgk_ref[...]) + bgk_ref[...]
        for h in range(GLA_HEADS):
            qk_cols = slice(h * GLA_DK, (h + 1) * GLA_DK)
            v_cols = slice(h * GLA_DV, (h + 1) * GLA_DV)
            o = _gla_head_chunk(q_ref[rows, qk_cols].astype(jnp.float32) * (GLA_DK ** -0.5),
                                k_ref[rows, qk_cols].astype(jnp.float32), v_ref[rows, v_cols], x_all[:, qk_cols],
                                gg_ref[rows, v_cols].astype(jnp.float32), gnorm, st_ref.at[h], consts)
            o_ref[rows, v_cols] = o.astype(o_ref.dtype)
        return carry

    lax.fori_loop(0, n_chunks, chunk, 0)


def _gla(proj, wgk_pad, b_gk, gla_norm, bt=512):
    t = proj.shape[0]
    kern = functools.partial(_gla_kernel, n_chunks=bt // GLA_CHUNK)
    full = lambda a: pl.BlockSpec(a.shape, lambda i: (0,) * a.ndim)
    return pl.pallas_call(
        kern,
        grid=(t // bt,),
        in_specs=[pl.BlockSpec((bt, GLA_QK_W), lambda i: (i, COL_GQ)),
                  pl.BlockSpec((bt, GLA_QK_W), lambda i: (i, COL_GK)),
                  pl.BlockSpec((bt, GLA_V_W), lambda i: (i, COL_GV)),
                  pl.BlockSpec((bt, GLA_V_W), lambda i: (i, COL_GG)),
                  pl.BlockSpec((bt, LANES), lambda i: (i, COL_LOW)),
                  full(wgk_pad), full(b_gk), full(gla_norm)],
        out_specs=pl.BlockSpec((bt, GLA_V_W), lambda i: (i, 0)),
        out_shape=jax.ShapeDtypeStruct((t, GLA_V_W), jnp.bfloat16),
        scratch_shapes=[pltpu.VMEM((GLA_HEADS, GLA_DV, GLA_DK), jnp.float32)],
        compiler_params=_cparams(("arbitrary",)),
        name="gla",
    )(proj, proj, proj, proj, proj, wgk_pad, b_gk, gla_norm)


def _rope(x, cos_full, sin_signed):
    return x * cos_full + pltpu.roll(x, MOBA_HD // 2, axis=1) * sin_signed


def _moba_prep_kernel(q_ref, k_ref, v_ref, cos_ref, sin_ref, qt_ref, ka_ref, vt_ref, km_ref):
    bt = q_ref.shape[0]
    n_blk = bt // MOBA_BLOCK
    cos_full, sin_signed = cos_ref[...], sin_ref[...]
    q = _rope(q_ref[...].astype(jnp.float32), cos_full, sin_signed) * (MOBA_HD ** -0.5 * LOG2_E)
    k = _rope(k_ref[...].astype(jnp.float32), cos_full, sin_signed)
    qt_ref[0] = q.T.astype(qt_ref.dtype)
    row_block = (lax.shift_right_logical(lax.broadcasted_iota(jnp.int32, (bt, MOBA_HD), 0), MOBA_BLOCK_SHIFT)
                 + pl.program_id(0) * n_blk)
    lane = lax.broadcasted_iota(jnp.int32, (bt, MOBA_HD), 1)
    one_hot = jnp.where(lane == row_block, 1.0, 0.0).astype(ka_ref.dtype)
    ka_ref[0] = jnp.concatenate([k.astype(ka_ref.dtype), one_hot], axis=1)
    km_ref[0] = jnp.mean(k.reshape(n_blk, MOBA_BLOCK, MOBA_HD), axis=1)
    v = v_ref[...].astype(jnp.float32)
    ones = jnp.ones((MOBA_VT_ROWS - MOBA_HD, MOBA_BLOCK), jnp.float32)
    for b in range(n_blk):
        v_t = v[b * MOBA_BLOCK:(b + 1) * MOBA_BLOCK].T
        vt_ref[0, b] = jnp.concatenate([v_t, ones], axis=0).astype(vt_ref.dtype)


def _moba_prep(proj, cos_full, sin_signed, bt=2048):
    t = proj.shape[0]
    nb = t // MOBA_BLOCK
    n_blk = bt // MOBA_BLOCK
    col = lambda c: pl.BlockSpec((bt, MOBA_HD), lambda i, h: (i, c + h))
    tab = pl.BlockSpec((bt, MOBA_HD), lambda i, h: (i, 0))
    return pl.pallas_call(
        _moba_prep_kernel,
        grid=(t // bt, MOBA_HEADS),
        in_specs=[col(COL_MQ), col(COL_MK), col(COL_MV), tab, tab],
        out_specs=[pl.BlockSpec((1, MOBA_HD, bt), lambda i, h: (h, 0, i)),
                   pl.BlockSpec((1, bt, 2 * MOBA_HD), lambda i, h: (h, i, 0)),
                   pl.BlockSpec((1, n_blk, MOBA_VT_ROWS, MOBA_BLOCK), lambda i, h: (h, i, 0, 0)),
                   pl.BlockSpec((1, n_blk, MOBA_HD), lambda i, h: (h, i, 0))],
        out_shape=[jax.ShapeDtypeStruct((MOBA_HEADS, MOBA_HD, t), jnp.bfloat16),
                   jax.ShapeDtypeStruct((MOBA_HEADS, t, 2 * MOBA_HD), jnp.bfloat16),
                   jax.ShapeDtypeStruct((MOBA_HEADS, nb, MOBA_VT_ROWS, MOBA_BLOCK), jnp.bfloat16),
                   jax.ShapeDtypeStruct((MOBA_HEADS, nb, MOBA_HD), jnp.float32)],
        compiler_params=_cparams(("parallel", "parallel")),
        name="moba_prep",
    )(proj, proj, proj, cos_full, sin_signed)


def _moba_kernel(qt_ref, k_ref, vt_ref, km_ref, o_ref, m_ref, acc_ref, sa_ref, sb_ref, *, nb):
    blk, grp, hd = MOBA_BLOCK, MOBA_GROUP * MOBA_BLOCK, MOBA_HD
    i = pl.program_id(1)
    q_t = qt_ref[0]

    km_hi, km_lo = _split_bf16(km_ref[0])
    gate = _dot(km_hi, q_t) + _dot(km_lo, q_t)
    sub = lax.broadcasted_iota(jnp.int32, (nb, blk), 0).astype(jnp.float32)
    gate = jnp.where(sub < i.astype(jnp.float32), gate, NEG)
    bias_t = jnp.full((nb, blk), NEG, jnp.float32)
    for _ in range(MOBA_TOPK):
        m = jnp.max(gate, axis=0, keepdims=True)
        idx = jnp.min(jnp.where(gate == m, sub, float(nb)), axis=0, keepdims=True)
        hit = sub == idx
        bias_t = jnp.where(hit & (m > 0.5 * NEG), 0.0, bias_t)
        gate = jnp.where(hit, NEG, gate)
    q_aug_t = jnp.concatenate(
        [q_t, bias_t.astype(jnp.bfloat16), jnp.zeros((hd - nb, blk), jnp.bfloat16)], axis=0)

    r0 = pl.multiple_of(i * blk, blk)
    s = _dot(k_ref[0, pl.ds(r0, blk), 0:hd], q_t)
    key = lax.broadcasted_iota(jnp.int32, (blk, blk), 0)
    qry = lax.broadcasted_iota(jnp.int32, (blk, blk), 1)
    s = jnp.where(key <= qry, s, NEG)
    m0 = jnp.max(s, axis=0, keepdims=True)
    m_ref[...] = m0
    acc_ref[...] = _dot(vt_ref[0, i], jnp.exp2((s - m0).astype(jnp.bfloat16)))

    last_group = nb // MOBA_GROUP - 1

    half = grp // 2

    def scores(g, s_ref):
        c0 = pl.multiple_of(jnp.minimum(g, last_group) * grp, grp)
        s_ref[0:half, :] = _dot(k_ref[0, pl.ds(c0, half), :], q_aug_t)
        s_ref[half:grp, :] = _dot(k_ref[0, pl.ds(c0 + half, half), :], q_aug_t)

    def accumulate(g, s_ref):
        s_g = s_ref[...]
        m_run = m_ref[...]
        m_new = jnp.maximum(m_run, jnp.max(s_g, axis=0, keepdims=True))
        alpha = jnp.exp2(m_run - m_new)
        p_g = jnp.exp2((s_g - m_new).astype(jnp.bfloat16))
        pv = [_dot(vt_ref[0, g * MOBA_GROUP + b], p_g[b * blk:(b + 1) * blk]) for b in range(MOBA_GROUP)]
        acc_ref[...] = alpha * acc_ref[...] + ((pv[0] + pv[1]) + (pv[2] + pv[3]))
        m_ref[...] = m_new

    scores(0, sa_ref)

    def past_pair(k, carry):
        scores(2 * k + 1, sb_ref)
        accumulate(2 * k, sa_ref)
        scores(2 * k + 2, sa_ref)
        accumulate(2 * k + 1, sb_ref)
        return carry

    n_groups = (i + MOBA_GROUP - 1) // MOBA_GROUP
    lax.fori_loop(0, (n_groups + 1) // 2, past_pair, 0)
    o_ref[...] = (acc_ref[0:hd, :] / acc_ref[hd:hd + 1, :]).T.astype(o_ref.dtype)


def _moba(q_t, k_aug, v_t, kmean):
    t = k_aug.shape[1]
    nb = t // MOBA_BLOCK
    assert nb <= MOBA_HD and (nb // MOBA_GROUP) % 2 == 0
    return pl.pallas_call(
        functools.partial(_moba_kernel, nb=nb),
        grid=(MOBA_HEADS, nb),
        in_specs=[pl.BlockSpec((1, MOBA_HD, MOBA_BLOCK), lambda h, i: (h, 0, i)),
                  pl.BlockSpec((1, t, 2 * MOBA_HD), lambda h, i: (h, 0, 0)),
                  pl.BlockSpec((1, nb, MOBA_VT_ROWS, MOBA_BLOCK), lambda h, i: (h, 0, 0, 0)),
                  pl.BlockSpec((1, nb, MOBA_HD), lambda h, i: (h, 0, 0))],
        out_specs=pl.BlockSpec((MOBA_BLOCK, MOBA_HD), lambda h, i: (i, h)),
        out_shape=jax.ShapeDtypeStruct((t, MOBA_W), jnp.bfloat16),
        scratch_shapes=[pltpu.VMEM((1, MOBA_BLOCK), jnp.float32),
                        pltpu.VMEM((MOBA_VT_ROWS, MOBA_BLOCK), jnp.float32),
                        pltpu.VMEM((MOBA_GROUP * MOBA_BLOCK, MOBA_BLOCK), jnp.float32),
                        pltpu.VMEM((MOBA_GROUP * MOBA_BLOCK, MOBA_BLOCK), jnp.float32)],
        compiler_params=_cparams(("parallel", "arbitrary")),
        name="moba_attention",
    )(q_t, k_aug, v_t, kmean)


def _route_rows(logit_rows, bias_ref):
    mx = functools.reduce(jnp.maximum, logit_rows)
    ex = [jnp.exp(l - mx) for l in logit_rows]
    inv_den = 1.0 / functools.reduce(lambda a, b: a + b, ex)
    aff = [e * inv_den for e in ex]
    biased = [a + bias_ref[e] for e, a in enumerate(aff)]

    def top2_sum(a0, a1, a2, a3):
        hi01, lo01 = jnp.maximum(a0, a1), jnp.minimum(a0, a1)
        hi23, lo23 = jnp.maximum(a2, a3), jnp.minimum(a2, a3)
        return jnp.maximum(hi01, hi23) + jnp.maximum(jnp.minimum(hi01, hi23), jnp.maximum(lo01, lo23))

    g = EXPERTS_PER_GROUP
    scores = [top2_sum(*biased[gi * g:(gi + 1) * g]) for gi in range(N_GROUPS)]
    best, best_score = jnp.zeros_like(scores[0]), scores[0]
    for gi in range(1, N_GROUPS):
        better = scores[gi] > best_score
        best = jnp.where(better, float(gi), best)
        best_score = jnp.where(better, scores[gi], best_score)

    def pick(rows, slot):
        out = rows[slot]
        for gi in range(1, N_GROUPS):
            out = jnp.where(best == float(gi), rows[gi * g + slot], out)
        return out

    vb = [pick(biased, s) for s in range(g)]
    va = [pick(aff, s) for s in range(g)]

    def argmax_first(vals):
        bi, bv = jnp.zeros_like(vals[0]), vals[0]
        for s in range(1, g):
            better = vals[s] > bv
            bi = jnp.where(better, float(s), bi)
            bv = jnp.where(better, vals[s], bv)
        return bi

    i1 = argmax_first(vb)
    i2 = argmax_first([jnp.where(i1 == float(s), -jnp.inf, vb[s]) for s in range(g)])

    def take(vals, idx):
        out = vals[0]
        for s in range(1, g):
            out = jnp.where(idx == float(s), vals[s], out)
        return out

    a1, a2 = take(va, i1), take(va, i2)
    inv = 1.0 / (a1 + a2)
    return best * g + i1, best * g + i2, a1 * inv, a2 * inv


def _mix_out_kernel(og_ref, om_ref, wo_ref, res_ref, g_ref, b_ref, wr_ref, br_ref,
                    h_ref, hb3_ref, route_ref, route_t_ref):
    y = _dot(og_ref[...], wo_ref[0:GLA_V_W, :]) + _dot(om_ref[...], wo_ref[GLA_V_W:, :])
    h = _layer_norm(DN_ALPHA * res_ref[...] + y, g_ref[...], b_ref[...])
    h_ref[...] = h
    tm = h.shape[0]
    hb3_ref[...] = h.astype(hb3_ref.dtype).reshape(tm, ROW_TILES, LANES)

    h_hi, h_lo = _split_bf16(h)
    wr = wr_ref[...]
    part_hi = _dot(h_hi, wr).T
    part_lo = _dot(h_lo, wr).T
    e = N_EXPERTS
    logits_t = part_hi[0:e] + part_hi[e:2 * e] + part_lo[0:e]
    e1, e2, w1, w2 = _route_rows([logits_t[x:x + 1, :] for x in range(e)], br_ref)

    sub8 = lax.broadcasted_iota(jnp.int32, (8, tm), 0)
    route = jnp.where(sub8 == 0, e1, jnp.where(sub8 == 1, e2, jnp.where(sub8 == 2, w1, w2)))
    route_ref[...] = route
    route_t_ref[...] = jnp.concatenate([route, jnp.zeros((LANES - 8, tm), jnp.float32)], axis=0).T


def _mix_out(o_gla, o_moba, w_o, h_res, ln_g, ln_b, w_route, b_route, tm=256):
    t = h_res.shape[0]
    row = lambda w: pl.BlockSpec((tm, w), lambda i: (i, 0))
    full = lambda a: pl.BlockSpec(a.shape, lambda i: (0,) * a.ndim)
    return pl.pallas_call(
        _mix_out_kernel,
        grid=(t // tm,),
        in_specs=[row(GLA_V_W), row(MOBA_W), full(w_o), row(D_MODEL), full(ln_g), full(ln_b), full(w_route),
                  pl.BlockSpec(memory_space=pltpu.SMEM)],
        out_specs=[row(D_MODEL), pl.BlockSpec((tm, ROW_TILES, LANES), lambda i: (i, 0, 0)),
                   pl.BlockSpec((8, tm), lambda i: (0, i)), row(LANES)],
        out_shape=[jax.ShapeDtypeStruct((t, D_MODEL), jnp.float32),
                   jax.ShapeDtypeStruct((t, ROW_TILES, LANES), jnp.bfloat16),
                   jax.ShapeDtypeStruct((8, t), jnp.float32),
                   jax.ShapeDtypeStruct((t, LANES), jnp.float32)],
        compiler_params=_cparams(("parallel",)),
        name="mix_out_ln_router",
    )(o_gla, o_moba, w_o, h_res, ln_g, ln_b, w_route, b_route)


def _moe_plan_kernel(route_ref, plan_ref, cnt_ref, carry_ref):
    @pl.when(pl.program_id(0) == 0)
    def _():
        carry_ref[...] = jnp.zeros_like(carry_ref)

    tm = route_ref.shape[1]
    e1, e2 = route_ref[0:1, :], route_ref[1:2, :]
    sub = lax.broadcasted_iota(jnp.int32, (N_EXPERTS, tm), 0).astype(jnp.float32)
    is1, is2 = sub == e1, sub == e2
    member = jnp.where(is1, 1.0, 0.0) + jnp.where(is2, 1.0, 0.0)
    row = lax.broadcasted_iota(jnp.int32, (tm, tm), 0)
    col = lax.broadcasted_iota(jnp.int32, (tm, tm), 1)
    earlier = jnp.where(row < col, 1.0, 0.0).astype(jnp.bfloat16)
    prefix = _dot(member.astype(jnp.bfloat16), earlier) + carry_ref[:, 0:1]
    r1 = jnp.sum(jnp.where(is1, prefix, 0.0), axis=0, keepdims=True)
    r2 = jnp.sum(jnp.where(is2, prefix, 0.0), axis=0, keepdims=True)
    carry_ref[...] += jnp.sum(member, axis=1, keepdims=True)
    sub8 = lax.broadcasted_iota(jnp.int32, (8, tm), 0)
    plan = jnp.where(sub8 == 0, e1, jnp.where(sub8 == 1, e2, jnp.where(sub8 == 2, r1, r2)))
    plan_ref[...] = plan.astype(jnp.int32)
    cnt_ref[...] = carry_ref[...]


def _moe_plan(route, tm=512):
    t = route.shape[1]
    return pl.pallas_call(
        _moe_plan_kernel,
        grid=(t // tm,),
        in_specs=[pl.BlockSpec((8, tm), lambda i: (0, i))],
        out_specs=[pl.BlockSpec((8, tm), lambda i: (0, i)), pl.BlockSpec((N_EXPERTS, LANES), lambda i: (0, 0))],
        out_shape=[jax.ShapeDtypeStruct((8, t), jnp.int32),
                   jax.ShapeDtypeStruct((N_EXPERTS, LANES), jnp.float32)],
        scratch_shapes=[pltpu.VMEM((N_EXPERTS, LANES), jnp.float32)],
        compiler_params=_cparams(("arbitrary",)),
        name="moe_plan",
    )(route)


def _sorted_rows(offs_ref, plan_ref, t):
    return (offs_ref[plan_ref[0, t]] + plan_ref[2, t], offs_ref[plan_ref[1, t]] + plan_ref[3, t])


def _moe_dispatch_kernel(offs_ref, plan_ref, h3_ref, xs_in_ref, xs_ref, sem):
    del xs_in_ref
    tm = plan_ref.shape[1]

    def copies(t):
        return [pltpu.make_async_copy(h3_ref.at[t], xs_ref.at[p], sem) for p in _sorted_rows(offs_ref, plan_ref, t)]

    def start(t, c):
        for cp in copies(t):
            cp.start()
        return c

    def wait(t, c):
        for cp in copies(t):
            cp.wait()
        return c

    lax.fori_loop(0, tm, start, 0)
    lax.fori_loop(0, tm, wait, 0)


def _moe_dispatch(offs, plan, h3, n_rows, tm=512):
    t = h3.shape[0]
    xs0 = jnp.zeros((n_rows, ROW_TILES, LANES), h3.dtype)
    any_spec = pl.BlockSpec(memory_space=pl.ANY)
    return pl.pallas_call(
        _moe_dispatch_kernel,
        grid_spec=pltpu.PrefetchScalarGridSpec(
            num_scalar_prefetch=1,
            grid=(t // tm,),
            in_specs=[pl.BlockSpec((8, tm), lambda i, offs: (0, i), memory_space=pltpu.SMEM),
                      pl.BlockSpec((tm, ROW_TILES, LANES), lambda i, offs: (i, 0, 0)), any_spec],
            out_specs=any_spec,
            scratch_shapes=[pltpu.SemaphoreType.DMA(())]),
        out_shape=jax.ShapeDtypeStruct(xs0.shape, xs0.dtype),
        input_output_aliases={3: 0},
        compiler_params=_cparams(("arbitrary",)),
        name="moe_dispatch",
    )(offs, plan, h3, xs0)


def _moe_group_kernel(tile_expert_ref, n_active_ref, x3_ref, wg_ref, wu_ref, wd_ref, o3_ref):
    del tile_expert_ref
    r = pl.program_id(0)
    tm = x3_ref.shape[0]

    @pl.when(r < n_active_ref[0])
    def _():
        x = x3_ref[...].reshape(tm, D_MODEL)
        hidden = (_silu(_dot(x, wg_ref[0])) * _dot(x, wu_ref[0])).astype(jnp.bfloat16)
        o3_ref[...] = _dot(hidden, wd_ref[0]).astype(o3_ref.dtype).reshape(tm, ROW_TILES, LANES)

    @pl.when(r >= n_active_ref[0])
    def _():
        o3_ref[...] = jnp.zeros_like(o3_ref)


def _moe_group(tile_expert, n_active, xs3, w_gate, w_up, w_down):
    n_rows = xs3.shape[0]
    tile = pl.BlockSpec((MOE_TM, ROW_TILES, LANES), lambda r, te, na: (r, 0, 0))
    return pl.pallas_call(
        _moe_group_kernel,
        grid_spec=pltpu.PrefetchScalarGridSpec(
            num_scalar_prefetch=2,
            grid=(n_rows // MOE_TM,),
            in_specs=[tile,
                      pl.BlockSpec((1, D_MODEL, D_EXPERT), lambda r, te, na: (te[r], 0, 0)),
                      pl.BlockSpec((1, D_MODEL, D_EXPERT), lambda r, te, na: (te[r], 0, 0)),
                      pl.BlockSpec((1, D_EXPERT, D_MODEL), lambda r, te, na: (te[r], 0, 0))],
            out_specs=tile),
        out_shape=jax.ShapeDtypeStruct(xs3.shape, jnp.bfloat16),
        compiler_params=_cparams(("arbitrary",)),
        name="moe_grouped_swiglu",
    )(tile_expert, n_active, xs3, w_gate, w_up, w_down)


def _moe_combine_kernel(offs_ref, plan_ref, o3_ref, route_t_ref, res_ref, g_ref, b_ref, h_ref, hb_ref,
                        buf1, buf2, sem):
    tm = plan_ref.shape[1]

    def copies(t):
        p1, p2 = _sorted_rows(offs_ref, plan_ref, t)
        return [pltpu.make_async_copy(o3_ref.at[p1], buf1.at[t], sem),
                pltpu.make_async_copy(o3_ref.at[p2], buf2.at[t], sem)]

    def start(t, c):
        for cp in copies(t):
            cp.start()
        return c

    def wait(t, c):
        for cp in copies(t):
            cp.wait()
        return c

    lax.fori_loop(0, tm, start, 0)
    lax.fori_loop(0, tm, wait, 0)
    route_t = route_t_ref[...]
    y = (route_t[:, 2:3] * buf1[...].reshape(tm, D_MODEL).astype(jnp.float32)
         + route_t[:, 3:4] * buf2[...].reshape(tm, D_MODEL).astype(jnp.float32))
    h = _layer_norm(DN_ALPHA * res_ref[...] + y, g_ref[...], b_ref[...])
    h_ref[...] = h
    hb_ref[...] = h.astype(hb_ref.dtype)


def _moe_combine(offs, plan, o3, route_t, h_res, ln_g, ln_b, tm=256):
    t = h_res.shape[0]
    row = lambda w: pl.BlockSpec((tm, w), lambda i, offs: (i, 0))
    vec = pl.BlockSpec((1, D_MODEL), lambda i, offs: (0, 0))
    return pl.pallas_call(
        _moe_combine_kernel,
        grid_spec=pltpu.PrefetchScalarGridSpec(
            num_scalar_prefetch=1,
            grid=(t // tm,),
            in_specs=[pl.BlockSpec((8, tm), lambda i, offs: (0, i), memory_space=pltpu.SMEM),
                      pl.BlockSpec(memory_space=pl.ANY), row(LANES), row(D_MODEL), vec, vec],
            out_specs=[row(D_MODEL), row(D_MODEL)],
            scratch_shapes=[pltpu.VMEM((tm, ROW_TILES, LANES), jnp.bfloat16),
                            pltpu.VMEM((tm, ROW_TILES, LANES), jnp.bfloat16),
                            pltpu.SemaphoreType.DMA(())]),
        out_shape=[jax.ShapeDtypeStruct((t, D_MODEL), jnp.float32),
                   jax.ShapeDtypeStruct((t, D_MODEL), jnp.bfloat16)],
        compiler_params=_cparams(("arbitrary",)),
        name="moe_combine_ln",
    )(offs, plan, o3, route_t, h_res, ln_g, ln_b)


def _moe(h, hb3, route, route_t, w_gate, w_up, w_down, ln_g, ln_b):
    t = h.shape[0]
    plan, cnt = _moe_plan(route)
    counts = cnt[:, 0].astype(jnp.int32)
    tiles_per_expert = (counts + MOE_TM - 1) // MOE_TM
    tile_end = jnp.cumsum(tiles_per_expert)
    offs = ((tile_end - tiles_per_expert) * MOE_TM).astype(jnp.int32)
    n_tiles = 2 * t // MOE_TM + N_EXPERTS
    tile_expert = jnp.minimum(
        jnp.sum(jnp.arange(n_tiles, dtype=jnp.int32)[:, None] >= tile_end[None, :], axis=1), N_EXPERTS - 1
    ).astype(jnp.int32)
    xs3 = _moe_dispatch(offs, plan, hb3, n_tiles * MOE_TM)
    o3 = _moe_group(tile_expert, tile_end[-1:].astype(jnp.int32), xs3, w_gate, w_up, w_down)
    return _moe_combine(offs, plan, o3, route_t, h, ln_g, ln_b)


def _rope_tables(positions):
    inv = 1.0 / (ROPE_THETA ** (jnp.arange(0, MOBA_HD, 2, dtype=jnp.float32) / MOBA_HD))
    ang = positions.astype(jnp.float32)[:, None] * inv
    cos, sin = jnp.cos(ang), jnp.sin(ang)
    return jnp.concatenate([cos, cos], axis=-1), jnp.concatenate([-sin, sin], axis=-1)


def _pack_w_in(w_in_l):
    pad = jnp.zeros((D_MODEL, LANES - GLA_GATE_RANK), w_in_l.dtype)
    low = w_in_l[:, LOW_OFF:LOW_OFF + GLA_GATE_RANK]
    return jnp.concatenate([w_in_l[:, :LOW_OFF], w_in_l[:, LOW_OFF + GLA_GATE_RANK:], low, pad],
                           axis=1).astype(jnp.bfloat16)


def kernel(x, positions, w_in, w_gk, b_gk, gla_norm, w_o, ln1_g, ln1_b, w_router, b_router,
           w_gate, w_up, w_down, ln2_g, ln2_b):
    batch, seq, _ = x.shape
    assert batch == 1 and seq % 512 == 0
    cos_full, sin_signed = _rope_tables(positions[0])
    wr_hi, wr_lo = _split_bf16(w_router)
    w_route = jnp.concatenate(
        [wr_hi, wr_lo, jnp.zeros((D_MODEL, LANES - 2 * N_EXPERTS), jnp.bfloat16)], axis=1)

    h = x[0]
    h_bf16 = h.astype(jnp.bfloat16)
    for l in range(DEPTH):
        wgk_pad = jnp.concatenate(
            [w_gk[l], jnp.zeros((LANES - GLA_GATE_RANK, GLA_QK_W), w_gk.dtype)], axis=0).astype(jnp.bfloat16)
        proj = _in_projection(h_bf16, _pack_w_in(w_in[l]))
        o_gla = _gla(proj, wgk_pad, b_gk[l][None, :], gla_norm[l][None, :])
        o_moba = _moba(*_moba_prep(proj, cos_full, sin_signed))
        h, hb3, route, route_t = _mix_out(o_gla, o_moba, w_o[l].astype(jnp.bfloat16), h,
                                          ln1_g[l][None, :], ln1_b[l][None, :], w_route, b_router)
        h, h_bf16 = _moe(h, hb3, route, route_t, w_gate[l].astype(jnp.bfloat16), w_up[l].astype(jnp.bfloat16),
                         w_down[l].astype(jnp.bfloat16), ln2_g[l][None, :], ln2_b[l][None, :])
    return h[None]
```

```python
import functools
import math

import jax
import jax.numpy as jnp
from jax import lax
from jax.experimental import pallas as pl
from jax.experimental.pallas import tpu as pltpu

D_MODEL = 2048
DEPTH = 2
GLA_HEADS = 4
GLA_DK = 128
GLA_DV = 256
GLA_GATE_RANK = 16
GLA_GATE_NORMALIZER = 16.0
MOBA_HEADS = 8
MOBA_HD = 128
MOBA_BLOCK = 256
MOBA_TOPK = 3
MOBA_BLOCK_SHIFT = 8
MOBA_GROUP = 4
MOBA_VT_ROWS = MOBA_HD + 16
LOG2_E = 1.4426950408889634
ROPE_THETA = 10000.0
GLA_QK_W = GLA_HEADS * GLA_DK
GLA_V_W = GLA_HEADS * GLA_DV
MOBA_W = MOBA_HEADS * MOBA_HD
N_EXPERTS = 16
N_GROUPS = 4
EXPERTS_PER_GROUP = N_EXPERTS // N_GROUPS
D_EXPERT = 1024
DN_ALPHA = (2.0 * DEPTH) ** 0.25
LN_EPS = 1e-5
RMS_EPS = 1e-6
NEG = -1e30

LANES = 128
GLA_CHUNK = 64
GLA_SUB = 16
VMEM_LIMIT = 48 * 1024 * 1024
ROW_TILES = D_MODEL // LANES
MOE_TM = 256

PROJ_MAIN_W = 2 * GLA_QK_W + 2 * GLA_V_W + 3 * MOBA_W
PROJ_W = PROJ_MAIN_W + LANES
LOW_OFF = 2 * GLA_QK_W + 2 * GLA_V_W
PROJ_TN = 896
COL_GQ, COL_GK = 0, 1
COL_GV = (2 * GLA_QK_W) // GLA_V_W
COL_GG = (2 * GLA_QK_W + GLA_V_W) // GLA_V_W
COL_MQ = LOW_OFF // MOBA_HD
COL_MK = COL_MQ + MOBA_HEADS
COL_MV = COL_MK + MOBA_HEADS
COL_LOW = PROJ_MAIN_W // LANES


def _cparams(semantics):
    return pltpu.CompilerParams(dimension_semantics=semantics, vmem_limit_bytes=VMEM_LIMIT)


def _split_bf16(x):
    hi = x.astype(jnp.bfloat16)
    lo = (x - hi.astype(jnp.float32)).astype(jnp.bfloat16)
    return hi, lo


def _dot(a, b):
    return jnp.dot(a, b, preferred_element_type=jnp.float32)


def _dot_nt(a, b):
    return lax.dot_general(a, b, (((1,), (1,)), ((), ())), preferred_element_type=jnp.float32)


def _layer_norm(y, g, b):
    mu = jnp.mean(y, axis=-1, keepdims=True)
    yc = y - mu
    var = jnp.mean(yc * yc, axis=-1, keepdims=True)
    return yc * lax.rsqrt(var + LN_EPS) * g + b


def _silu(x):
    return x * (1.0 / (1.0 + jnp.exp(-x)))


def _proj_kernel(x_ref, w_ref, o_ref):
    o_ref[...] = _dot(x_ref[...], w_ref[...]).astype(o_ref.dtype)


def _in_projection(h_bf16, w_packed, tm=1024):
    t = h_bf16.shape[0]
    return pl.pallas_call(
        _proj_kernel,
        grid=(PROJ_W // PROJ_TN, t // tm),
        in_specs=[pl.BlockSpec((tm, D_MODEL), lambda j, i: (i, 0)),
                  pl.BlockSpec((D_MODEL, PROJ_TN), lambda j, i: (0, j))],
        out_specs=pl.BlockSpec((tm, PROJ_TN), lambda j, i: (i, j)),
        out_shape=jax.ShapeDtypeStruct((t, PROJ_W), jnp.bfloat16),
        compiler_params=_cparams(("parallel", "parallel")),
        name="in_projection",
    )(h_bf16, w_packed)


def _gla_head_chunk(q, k, v, x, gate, gnorm, st_ref, consts):
    tril, sub_row, lane_c = consts
    c_len, sub = GLA_CHUNK, GLA_SUB
    g = (jnp.minimum(x, 0.0) - jnp.log1p(jnp.exp(-jnp.abs(x)))) * (1.0 / GLA_GATE_NORMALIZER)
    g_hi, g_lo = _split_bf16(g)
    b = _dot(tril, g_hi) + _dot(tril, g_lo)
    b_last = b[c_len - 1:c_len, :]

    st = st_ref[...]
    inter = _dot_nt((q * jnp.exp(b)).astype(jnp.bfloat16), st.astype(jnp.bfloat16))
    k_dec = (k * jnp.exp(b_last - b)).astype(jnp.bfloat16)
    st_ref[...] = st * jnp.exp(b_last) + _dot(v.T, k_dec)

    blocks = []
    for i_sub in range(c_len // sub):
        lo = i_sub * sub
        q_i, k_i, b_i = q[lo:lo + sub], k[lo:lo + sub], b[lo:lo + sub]
        a_i = jnp.zeros((sub, c_len), jnp.float32)
        for j in range(sub):
            e = jnp.exp(jnp.minimum(b_i - b_i[j:j + 1, :], 0.0))
            term = jnp.where(sub_row >= j, q_i * e * k_i[j:j + 1, :], 0.0)
            a_i = jnp.where(lane_c == lo + j, jnp.sum(term, axis=1, keepdims=True), a_i)
        if i_sub > 0:
            b_start = b[lo - 1:lo, :]
            q_s = (q_i * jnp.exp(b_i - b_start)).astype(jnp.bfloat16)
            k_s = (k * jnp.exp(jnp.minimum(b_start - b, 0.0))).astype(jnp.bfloat16)
            a_i = jnp.where(lane_c < lo, _dot_nt(q_s, k_s), a_i)
        blocks.append(a_i)
    attn = jnp.concatenate(blocks, axis=0).astype(jnp.bfloat16)
    o = inter + _dot(attn, v)
    o = o * lax.rsqrt(jnp.mean(o * o, axis=-1, keepdims=True) + RMS_EPS)
    return o * gnorm * _silu(gate)


def _gla_kernel(q_ref, k_ref, v_ref, gg_ref, low_ref, wgk_ref, bgk_ref, norm_ref, o_ref, st_ref, *, n_chunks):
    c_len, sub = GLA_CHUNK, GLA_SUB

    @pl.when(pl.program_id(0) == 0)
    def _():
        st_ref[...] = jnp.zeros_like(st_ref)

    row = lax.broadcasted_iota(jnp.int32, (c_len, c_len), 0)
    col = lax.broadcasted_iota(jnp.int32, (c_len, c_len), 1)
    consts = (jnp.where(col <= row, 1.0, 0.0).astype(jnp.bfloat16),
              lax.broadcasted_iota(jnp.int32, (sub, GLA_DK), 0),
              lax.broadcasted_iota(jnp.int32, (sub, c_len), 1))
    gnorm = norm_ref[...]

    def chunk(c, carry):
        rows = pl.ds(pl.multiple_of(c * c_len, c_len), c_len)
        x_all = _dot(low_ref[rows, :], wgk_ref[...]) + bgk_ref[...]
        for h in range(GLA_HEADS):
            qk_cols = slice(h * GLA_DK, (h + 1) * GLA_DK)
            v_cols = slice(h * GLA_DV, (h + 1) * GLA_DV)
            o = _gla_head_chunk(q_ref[rows, qk_cols].astype(jnp.float32) * (GLA_DK ** -0.5),
                                k_ref[rows, qk_cols].astype(jnp.float32), v_ref[rows, v_cols], x_all[:, qk_cols],
                                gg_ref[rows, v_cols].astype(jnp.float32), gnorm, st_ref.at[h], consts)
            o_ref[rows, v_cols] = o.astype(o_ref.dtype)
        return carry

    lax.fori_loop(0, n_chunks, chunk, 0)


def _gla(proj, wgk_pad, b_gk, gla_norm, bt=512):
    t = proj.shape[0]
    kern = functools.partial(_gla_kernel, n_chunks=bt // GLA_CHUNK)
    full = lambda a: pl.BlockSpec(a.shape, lambda i: (0,) * a.ndim)
    return pl.pallas_call(
        kern,
        grid=(t // bt,),
        in_specs=[pl.BlockSpec((bt, GLA_QK_W), lambda i: (i, COL_GQ)),
                  pl.BlockSpec((bt, GLA_QK_W), lambda i: (i, COL_GK)),
                  pl.BlockSpec((bt, GLA_V_W), lambda i: (i, COL_GV)),
                  pl.BlockSpec((bt, GLA_V_W), lambda i: (i, COL_GG)),
                  pl.BlockSpec((bt, LANES), lambda i: (i, COL_LOW)),
                  full(wgk_pad), full(b_gk), full(gla_norm)],
        out_specs=pl.BlockSpec((bt, GLA_V_W), lambda i: (i, 0)),
        out_shape=jax.ShapeDtypeStruct((t, GLA_V_W), jnp.bfloat16),
        scratch_shapes=[pltpu.VMEM((GLA_HEADS, GLA_DV, GLA_DK), jnp.float32)],
        compiler_params=_cparams(("arbitrary",)),
        name="gla",
    )(proj, proj, proj, proj, proj, wgk_pad, b_gk, gla_norm)


def _rope(x, cos_full, sin_signed):
    return x * cos_full + pltpu.roll(x, MOBA_HD // 2, axis=1) * sin_signed


def _moba_prep_kernel(q_ref, k_ref, v_ref, cos_ref, sin_ref, qt_ref, ka_ref, vt_ref, km_ref):
    bt = q_ref.shape[0]
    n_blk = bt // MOBA_BLOCK
    cos_full, sin_signed = cos_ref[...], sin_ref[...]
    q = _rope(q_ref[...].astype(jnp.float32), cos_full, sin_signed) * (MOBA_HD ** -0.5 * LOG2_E)
    k = _rope(k_ref[...].astype(jnp.float32), cos_full, sin_signed)
    qt_ref[0] = q.T.astype(qt_ref.dtype)
    row_block = (lax.shift_right_logical(lax.broadcasted_iota(jnp.int32, (bt, MOBA_HD), 0), MOBA_BLOCK_SHIFT)
                 + pl.program_id(0) * n_blk)
    lane = lax.broadcasted_iota(jnp.int32, (bt, MOBA_HD), 1)
    one_hot = jnp.where(lane == row_block, 1.0, 0.0).astype(ka_ref.dtype)
    ka_ref[0] = jnp.concatenate([k.astype(ka_ref.dtype), one_hot], axis=1)
    km_ref[0] = jnp.mean(k.reshape(n_blk, MOBA_BLOCK, MOBA_HD), axis=1)
    v = v_ref[...].astype(jnp.float32)
    ones = jnp.ones((MOBA_VT_ROWS - MOBA_HD, MOBA_BLOCK), jnp.float32)
    for b in range(n_blk):
        v_t = v[b * MOBA_BLOCK:(b + 1) * MOBA_BLOCK].T
        vt_ref[0, b] = jnp.concatenate([v_t, ones], axis=0).astype(vt_ref.dtype)


def _moba_prep(proj, cos_full, sin_signed, bt=2048):
    t = proj.shape[0]
    nb = t // MOBA_BLOCK
    n_blk = bt // MOBA_BLOCK
    col = lambda c: pl.BlockSpec((bt, MOBA_HD), lambda i, h: (i, c + h))
    tab = pl.BlockSpec((bt, MOBA_HD), lambda i, h: (i, 0))
    return pl.pallas_call(
        _moba_prep_kernel,
        grid=(t // bt, MOBA_HEADS),
        in_specs=[col(COL_MQ), col(COL_MK), col(COL_MV), tab, tab],
        out_specs=[pl.BlockSpec((1, MOBA_HD, bt), lambda i, h: (h, 0, i)),
                   pl.BlockSpec((1, bt, 2 * MOBA_HD), lambda i, h: (h, i, 0)),
                   pl.BlockSpec((1, n_blk, MOBA_VT_ROWS, MOBA_BLOCK), lambda i, h: (h, i, 0, 0)),
                   pl.BlockSpec((1, n_blk, MOBA_HD), lambda i, h: (h, i, 0))],
        out_shape=[jax.ShapeDtypeStruct((MOBA_HEADS, MOBA_HD, t), jnp.bfloat16),
                   jax.ShapeDtypeStruct((MOBA_HEADS, t, 2 * MOBA_HD), jnp.bfloat16),
                   jax.ShapeDtypeStruct((MOBA_HEADS, nb, MOBA_VT_ROWS, MOBA_BLOCK), jnp.bfloat16),
                   jax.ShapeDtypeStruct((MOBA_HEADS, nb, MOBA_HD), jnp.float32)],
        compiler_params=_cparams(("parallel", "parallel")),
        name="moba_prep",
    )(proj, proj, proj, cos_full, sin_signed)


def _moba_kernel(qt_ref, k_ref, vt_ref, km_ref, o_ref, m_ref, acc_ref, sa_ref, sb_ref, *, nb):
    blk, grp, hd = MOBA_BLOCK, MOBA_GROUP * MOBA_BLOCK, MOBA_HD
    i = pl.program_id(1)
    q_t = qt_ref[0]

    km_hi, km_lo = _split_bf16(km_ref[0])
    gate = _dot(km_hi, q_t) + _dot(km_lo, q_t)
    sub = lax.broadcasted_iota(jnp.int32, (nb, blk), 0).astype(jnp.float32)
    gate = jnp.where(sub < i.astype(jnp.float32), gate, NEG)
    bias_t = jnp.full((nb, blk), NEG, jnp.float32)
    for _ in range(MOBA_TOPK):
        m = jnp.max(gate, axis=0, keepdims=True)
        idx = jnp.min(jnp.where(gate == m, sub, float(nb)), axis=0, keepdims=True)
        hit = sub == idx
        bias_t = jnp.where(hit & (m > 0.5 * NEG), 0.0, bias_t)
        gate = jnp.where(hit, NEG, gate)
    q_aug_t = jnp.concatenate(
        [q_t, bias_t.astype(jnp.bfloat16), jnp.zeros((hd - nb, blk), jnp.bfloat16)], axis=0)

    r0 = pl.multiple_of(i * blk, blk)
    s = _dot(k_ref[0, pl.ds(r0, blk), 0:hd], q_t)
    key = lax.broadcasted_iota(jnp.int32, (blk, blk), 0)
    qry = lax.broadcasted_iota(jnp.int32, (blk, blk), 1)
    s = jnp.where(key <= qry, s, NEG)
    m0 = jnp.max(s, axis=0, keepdims=True)
    m_ref[...] = m0
    acc_ref[...] = _dot(vt_ref[0, i], jnp.exp2((s - m0).astype(jnp.bfloat16)))

    last_group = nb // MOBA_GROUP - 1

    half = grp // 2

    def scores(g, s_ref):
        c0 = pl.multiple_of(jnp.minimum(g, last_group) * grp, grp)
        s_ref[0:half, :] = _dot(k_ref[0, pl.ds(c0, half), :], q_aug_t)
        s_ref[half:grp, :] = _dot(k_ref[0, pl.ds(c0 + half, half), :], q_aug_t)

    def accumulate(g, s_ref):
        s_g = s_ref[...]
        m_run = m_ref[...]
        m_new = jnp.maximum(m_run, jnp.max(s_g, axis=0, keepdims=True))
        alpha = jnp.exp2(m_run - m_new)
        p_g = jnp.exp2((s_g - m_new).astype(jnp.bfloat16))
        pv = [_dot(vt_ref[0, g * MOBA_GROUP + b], p_g[b * blk:(b + 1) * blk]) for b in range(MOBA_GROUP)]
        acc_ref[...] = alpha * acc_ref[...] + ((pv[0] + pv[1]) + (pv[2] + pv[3]))
        m_ref[...] = m_new

    scores(0, sa_ref)

    def past_pair(k, carry):
        scores(2 * k + 1, sb_ref)
        accumulate(2 * k, sa_ref)
        scores(2 * k + 2, sa_ref)
        accumulate(2 * k + 1, sb_ref)
        return carry

    n_groups = (i + MOBA_GROUP - 1) // MOBA_GROUP
    lax.fori_loop(0, (n_groups + 1) // 2, past_pair, 0)
    o_ref[...] = (acc_ref[0:hd, :] / acc_ref[hd:hd + 1, :]).T.astype(o_ref.dtype)


def _moba(q_t, k_aug, v_t, kmean):
    t = k_aug.shape[1]
    nb = t // MOBA_BLOCK
    assert nb <= MOBA_HD and (nb // MOBA_GROUP) % 2 == 0
    return pl.pallas_call(
        functools.partial(_moba_kernel, nb=nb),
        grid=(MOBA_HEADS, nb),
        in_specs=[pl.BlockSpec((1, MOBA_HD, MOBA_BLOCK), lambda h, i: (h, 0, i)),
                  pl.BlockSpec((1, t, 2 * MOBA_HD), lambda h, i: (h, 0, 0)),
                  pl.BlockSpec((1, nb, MOBA_VT_ROWS, MOBA_BLOCK), lambda h, i: (h, 0, 0, 0)),
                  pl.BlockSpec((1, nb, MOBA_HD), lambda h, i: (h, 0, 0))],
        out_specs=pl.BlockSpec((MOBA_BLOCK, MOBA_HD), lambda h, i: (i, h)),
        out_shape=jax.ShapeDtypeStruct((t, MOBA_W), jnp.bfloat16),
        scratch_shapes=[pltpu.VMEM((1, MOBA_BLOCK), jnp.float32),
                        pltpu.VMEM((MOBA_VT_ROWS, MOBA_BLOCK), jnp.float32),
                        pltpu.VMEM((MOBA_GROUP * MOBA_BLOCK, MOBA_BLOCK), jnp.float32),
                        pltpu.VMEM((MOBA_GROUP * MOBA_BLOCK, MOBA_BLOCK), jnp.float32)],
        compiler_params=_cparams(("parallel", "arbitrary")),
        name="moba_attention",
    )(q_t, k_aug, v_t, kmean)


def _route_rows(logit_rows, bias_ref):
    mx = functools.reduce(jnp.maximum, logit_rows)
    ex = [jnp.exp(l - mx) for l in logit_rows]
    inv_den = 1.0 / functools.reduce(lambda a, b: a + b, ex)
    aff = [e * inv_den for e in ex]
    biased = [a + bias_ref[e] for e, a in enumerate(aff)]

    def top2_sum(a0, a1, a2, a3):
        hi01, lo01 = jnp.maximum(a0, a1), jnp.minimum(a0, a1)
        hi23, lo23 = jnp.maximum(a2, a3), jnp.minimum(a2, a3)
        return jnp.maximum(hi01, hi23) + jnp.maximum(jnp.minimum(hi01, hi23), jnp.maximum(lo01, lo23))

    g = EXPERTS_PER_GROUP
    scores = [top2_sum(*biased[gi * g:(gi + 1) * g]) for gi in range(N_GROUPS)]
    best, best_score = jnp.zeros_like(scores[0]), scores[0]
    for gi in range(1, N_GROUPS):
        better = scores[gi] > best_score
        best = jnp.where(better, float(gi), best)
        best_score = jnp.where(better, scores[gi], best_score)

    def pick(rows, slot):
        out = rows[slot]
        for gi in range(1, N_GROUPS):
            out = jnp.where(best == float(gi), rows[gi * g + slot], out)
        return out

    vb = [pick(biased, s) for s in range(g)]
    va = [pick(aff, s) for s in range(g)]

    def argmax_first(vals):
        bi, bv = jnp.zeros_like(vals[0]), vals[0]
        for s in range(1, g):
            better = vals[s] > bv
            bi = jnp.where(better, float(s), bi)
            bv = jnp.where(better, vals[s], bv)
        return bi

    i1 = argmax_first(vb)
    i2 = argmax_first([jnp.where(i1 == float(s), -jnp.inf, vb[s]) for s in range(g)])

    def take(vals, idx):
        out = vals[0]
        for s in range(1, g):
            out = jnp.where(idx == float(s), vals[s], out)
        return out

    a1, a2 = take(va, i1), take(va, i2)
    inv = 1.0 / (a1 + a2)
    return best * g + i1, best * g + i2, a1 * inv, a2 * inv


def _mix_out_kernel(og_ref, om_ref, wo_ref, res_ref, g_ref, b_ref, wr_ref, br_ref,
                    h_ref, hb3_ref, route_ref, route_t_ref):
    y = _dot(og_ref[...], wo_ref[0:GLA_V_W, :]) + _dot(om_ref[...], wo_ref[GLA_V_W:, :])
    h = _layer_norm(DN_ALPHA * res_ref[...] + y, g_ref[...], b_ref[...])
    h_ref[...] = h
    tm = h.shape[0]
    hb3_ref[...] = h.astype(hb3_ref.dtype).reshape(tm, ROW_TILES, LANES)

    h_hi, h_lo = _split_bf16(h)
    wr = wr_ref[...]
    part_hi = _dot(h_hi, wr).T
    part_lo = _dot(h_lo, wr).T
    e = N_EXPERTS
    logits_t = part_hi[0:e] + part_hi[e:2 * e] + part_lo[0:e]
    e1, e2, w1, w2 = _route_rows([logits_t[x:x + 1, :] for x in range(e)], br_ref)

    sub8 = lax.broadcasted_iota(jnp.int32, (8, tm), 0)
    route = jnp.where(sub8 == 0, e1, jnp.where(sub8 == 1, e2, jnp.where(sub8 == 2, w1, w2)))
    route_ref[...] = route
    route_t_ref[...] = jnp.concatenate([route, jnp.zeros((LANES - 8, tm), jnp.float32)], axis=0).T


def _mix_out(o_gla, o_moba, w_o, h_res, ln_g, ln_b, w_route, b_route, tm=256):
    t = h_res.shape[0]
    row = lambda w: pl.BlockSpec((tm, w), lambda i: (i, 0))
    full = lambda a: pl.BlockSpec(a.shape, lambda i: (0,) * a.ndim)
    return pl.pallas_call(
        _mix_out_kernel,
        grid=(t // tm,),
        in_specs=[row(GLA_V_W), row(MOBA_W), full(w_o), row(D_MODEL), full(ln_g), full(ln_b), full(w_route),
                  pl.BlockSpec(memory_space=pltpu.SMEM)],
        out_specs=[row(D_MODEL), pl.BlockSpec((tm, ROW_TILES, LANES), lambda i: (i, 0, 0)),
                   pl.BlockSpec((8, tm), lambda i: (0, i)), row(LANES)],
        out_shape=[jax.ShapeDtypeStruct((t, D_MODEL), jnp.float32),
                   jax.ShapeDtypeStruct((t, ROW_TILES, LANES), jnp.bfloat16),
                   jax.ShapeDtypeStruct((8, t), jnp.float32),
                   jax.ShapeDtypeStruct((t, LANES), jnp.float32)],
        compiler_params=_cparams(("parallel",)),
        name="mix_out_ln_router",
    )(o_gla, o_moba, w_o, h_res, ln_g, ln_b, w_route, b_route)


def _moe_plan_kernel(route_ref, plan_ref, cnt_ref, carry_ref):
    @pl.when(pl.program_id(0) == 0)
    def _():
        carry_ref[...] = jnp.zeros_like(carry_ref)

    tm = route_ref.shape[1]
    e1, e2 = route_ref[0:1, :], route_ref[1:2, :]
    sub = lax.broadcasted_iota(jnp.int32, (N_EXPERTS, tm), 0).astype(jnp.float32)
    is1, is2 = sub == e1, sub == e2
    member = jnp.where(is1, 1.0, 0.0) + jnp.where(is2, 1.0, 0.0)
    row = lax.broadcasted_iota(jnp.int32, (tm, tm), 0)
    col = lax.broadcasted_iota(jnp.int32, (tm, tm), 1)
    earlier = jnp.where(row < col, 1.0, 0.0).astype(jnp.bfloat16)
    prefix = _dot(member.astype(jnp.bfloat16), earlier) + carry_ref[:, 0:1]
    r1 = jnp.sum(jnp.where(is1, prefix, 0.0), axis=0, keepdims=True)
    r2 = jnp.sum(jnp.where(is2, prefix, 0.0), axis=0, keepdims=True)
    carry_ref[...] += jnp.sum(member, axis=1, keepdims=True)
    sub8 = lax.broadcasted_iota(jnp.int32, (8, tm), 0)
    plan = jnp.where(sub8 == 0, e1, jnp.where(sub8 == 1, e2, jnp.where(sub8 == 2, r1, r2)))
    plan_ref[...] = plan.astype(jnp.int32)
    cnt_ref[...] = carry_ref[...]


def _moe_plan(route, tm=512):
    t = route.shape[1]
    return pl.pallas_call(
        _moe_plan_kernel,
        grid=(t // tm,),
        in_specs=[pl.BlockSpec((8, tm), lambda i: (0, i))],
        out_specs=[pl.BlockSpec((8, tm), lambda i: (0, i)), pl.BlockSpec((N_EXPERTS, LANES), lambda i: (0, 0))],
        out_shape=[jax.ShapeDtypeStruct((8, t), jnp.int32),
                   jax.ShapeDtypeStruct((N_EXPERTS, LANES), jnp.float32)],
        scratch_shapes=[pltpu.VMEM((N_EXPERTS, LANES), jnp.float32)],
        compiler_params=_cparams(("arbitrary",)),
        name="moe_plan",
    )(route)


def _sorted_rows(offs_ref, plan_ref, t):
    return (offs_ref[plan_ref[0, t]] + plan_ref[2, t], offs_ref[plan_ref[1, t]] + plan_ref[3, t])


def _moe_dispatch_kernel(offs_ref, plan_ref, h3_ref, xs_in_ref, xs_ref, sem):
    del xs_in_ref
    tm = plan_ref.shape[1]

    def copies(t):
        return [pltpu.make_async_copy(h3_ref.at[t], xs_ref.at[p], sem) for p in _sorted_rows(offs_ref, plan_ref, t)]

    def start(t, c):
        for cp in copies(t):
            cp.start()
        return c

    def wait(t, c):
        for cp in copies(t):
            cp.wait()
        return c

    lax.fori_loop(0, tm, start, 0)
    lax.fori_loop(0, tm, wait, 0)


def _moe_dispatch(offs, plan, h3, n_rows, tm=512):
    t = h3.shape[0]
    xs0 = jnp.zeros((n_rows, ROW_TILES, LANES), h3.dtype)
    any_spec = pl.BlockSpec(memory_space=pl.ANY)
    return pl.pallas_call(
        _moe_dispatch_kernel,
        grid_spec=pltpu.PrefetchScalarGridSpec(
            num_scalar_prefetch=1,
            grid=(t // tm,),
            in_specs=[pl.BlockSpec((8, tm), lambda i, offs: (0, i), memory_space=pltpu.SMEM),
                      pl.BlockSpec((tm, ROW_TILES, LANES), lambda i, offs: (i, 0, 0)), any_spec],
            out_specs=any_spec,
            scratch_shapes=[pltpu.SemaphoreType.DMA(())]),
        out_shape=jax.ShapeDtypeStruct(xs0.shape, xs0.dtype),
        input_output_aliases={3: 0},
        compiler_params=_cparams(("arbitrary",)),
        name="moe_dispatch",
    )(offs, plan, h3, xs0)


def _moe_group_kernel(tile_expert_ref, n_active_ref, x3_ref, wg_ref, wu_ref, wd_ref, o3_ref):
    del tile_expert_ref
    r = pl.program_id(0)
    tm = x3_ref.shape[0]

    @pl.when(r < n_active_ref[0])
    def _():
        x = x3_ref[...].reshape(tm, D_MODEL)
        hidden = (_silu(_dot(x, wg_ref[0])) * _dot(x, wu_ref[0])).astype(jnp.bfloat16)
        o3_ref[...] = _dot(hidden, wd_ref[0]).astype(o3_ref.dtype).reshape(tm, ROW_TILES, LANES)

    @pl.when(r >= n_active_ref[0])
    def _():
        o3_ref[...] = jnp.zeros_like(o3_ref)


def _moe_group(tile_expert, n_active, xs3, w_gate, w_up, w_down):
    n_rows = xs3.shape[0]
    tile = pl.BlockSpec((MOE_TM, ROW_TILES, LANES), lambda r, te, na: (r, 0, 0))
    return pl.pallas_call(
        _moe_group_kernel,
        grid_spec=pltpu.PrefetchScalarGridSpec(
            num_scalar_prefetch=2,
            grid=(n_rows // MOE_TM,),
            in_specs=[tile,
                      pl.BlockSpec((1, D_MODEL, D_EXPERT), lambda r, te, na: (te[r], 0, 0)),
                      pl.BlockSpec((1, D_MODEL, D_EXPERT), lambda r, te, na: (te[r], 0, 0)),
                      pl.BlockSpec((1, D_EXPERT, D_MODEL), lambda r, te, na: (te[r], 0, 0))],
            out_specs=tile),
        out_shape=jax.ShapeDtypeStruct(xs3.shape, jnp.bfloat16),
        compiler_params=_cparams(("arbitrary",)),
        name="moe_grouped_swiglu",
    )(tile_expert, n_active, xs3, w_gate, w_up, w_down)


def _moe_combine_kernel(offs_ref, plan_ref, o3_ref, route_t_ref, res_ref, g_ref, b_ref, h_ref, hb_ref,
                        buf1, buf2, sem):
    tm = plan_ref.shape[1]

    def copies(t):
        p1, p2 = _sorted_rows(offs_ref, plan_ref, t)
        return [pltpu.make_async_copy(o3_ref.at[p1], buf1.at[t], sem),
                pltpu.make_async_copy(o3_ref.at[p2], buf2.at[t], sem)]

    def start(t, c):
        for cp in copies(t):
            cp.start()
        return c

    def wait(t, c):
        for cp in copies(t):
            cp.wait()
        return c

    lax.fori_loop(0, tm, start, 0)
    lax.fori_loop(0, tm, wait, 0)
    route_t = route_t_ref[...]
    y = (route_t[:, 2:3] * buf1[...].reshape(tm, D_MODEL).astype(jnp.float32)
         + route_t[:, 3:4] * buf2[...].reshape(tm, D_MODEL).astype(jnp.float32))
    h = _layer_norm(DN_ALPHA * res_ref[...] + y, g_ref[...], b_ref[...])
    h_ref[...] = h
    hb_ref[...] = h.astype(hb_ref.dtype)


def _moe_combine(offs, plan, o3, route_t, h_res, ln_g, ln_b, tm=512):
    t = h_res.shape[0]
    row = lambda w: pl.BlockSpec((tm, w), lambda i, offs: (i, 0))
    vec = pl.BlockSpec((1, D_MODEL), lambda i, offs: (0, 0))
    return pl.pallas_call(
        _moe_combine_kernel,
        grid_spec=pltpu.PrefetchScalarGridSpec(
            num_scalar_prefetch=1,
            grid=(t // tm,),
            in_specs=[pl.BlockSpec((8, tm), lambda i, offs: (0, i), memory_space=pltpu.SMEM),
                      pl.BlockSpec(memory_space=pl.ANY), row(LANES), row(D_MODEL), vec, vec],
            out_specs=[row(D_MODEL), row(D_MODEL)],
            scratch_shapes=[pltpu.VMEM((tm, ROW_TILES, LANES), jnp.bfloat16),
                            pltpu.VMEM((tm, ROW_TILES, LANES), jnp.bfloat16),
                            pltpu.SemaphoreType.DMA(())]),
        out_shape=[jax.ShapeDtypeStruct((t, D_MODEL), jnp.float32),
                   jax.ShapeDtypeStruct((t, D_MODEL), jnp.bfloat16)],
        compiler_params=_cparams(("arbitrary",)),
        name="moe_combine_ln",
    )(offs, plan, o3, route_t, h_res, ln_g, ln_b)


def _moe(h, hb3, route, route_t, w_gate, w_up, w_down, ln_g, ln_b):
    t = h.shape[0]
    plan, cnt = _moe_plan(route)
    counts = cnt[:, 0].astype(jnp.int32)
    tiles_per_expert = (counts + MOE_TM - 1) // MOE_TM
    tile_end = jnp.cumsum(tiles_per_expert)
    offs = ((tile_end - tiles_per_expert) * MOE_TM).astype(jnp.int32)
    n_tiles = 2 * t // MOE_TM + N_EXPERTS
    tile_expert = jnp.minimum(
        jnp.sum(jnp.arange(n_tiles, dtype=jnp.int32)[:, None] >= tile_end[None, :], axis=1), N_EXPERTS - 1
    ).astype(jnp.int32)
    xs3 = _moe_dispatch(offs, plan, hb3, n_tiles * MOE_TM)
    o3 = _moe_group(tile_expert, tile_end[-1:].astype(jnp.int32), xs3, w_gate, w_up, w_down)
    return _moe_combine(offs, plan, o3, route_t, h, ln_g, ln_b)


def _rope_tables(positions):
    inv = 1.0 / (ROPE_THETA ** (jnp.arange(0, MOBA_HD, 2, dtype=jnp.float32) / MOBA_HD))
    ang = positions.astype(jnp.float32)[:, None] * inv
    cos, sin = jnp.cos(ang), jnp.sin(ang)
    return jnp.concatenate([cos, cos], axis=-1), jnp.concatenate([-sin, sin], axis=-1)


def _pack_w_in(w_in_l):
    pad = jnp.zeros((D_MODEL, LANES - GLA_GATE_RANK), w_in_l.dtype)
    low = w_in_l[:, LOW_OFF:LOW_OFF + GLA_GATE_RANK]
    return jnp.concatenate([w_in_l[:, :LOW_OFF], w_in_l[:, LOW_OFF + GLA_GATE_RANK:], low, pad],
                           axis=1).astype(jnp.bfloat16)


def kernel(x, positions, w_in, w_gk, b_gk, gla_norm, w_o, ln1_g, ln1_b, w_router, b_router,
           w_gate, w_up, w_down, ln2_g, ln2_b):
    batch, seq, _ = x.shape
    assert batch == 1 and seq % 512 == 0
    cos_full, sin_signed = _rope_tables(positions[0])
    wr_hi, wr_lo = _split_bf16(w_router)
    w_route = jnp.concatenate(
        [wr_hi, wr_lo, jnp.zeros((D_MODEL, LANES - 2 * N_EXPERTS), jnp.bfloat16)], axis=1)

    h = x[0]
    h_bf16 = h.astype(jnp.bfloat16)
    for l in range(DEPTH):
        wgk_pad = jnp.concatenate(
            [w_gk[l], jnp.zeros((LANES - GLA_GATE_RANK, GLA_QK_W), w_gk.dtype)], axis=0).astype(jnp.bfloat16)
        proj = _in_projection(h_bf16, _pack_w_in(w_in[l]))
        o_gla = _gla(proj, wgk_pad, b_gk[l][None, :], gla_norm[l][None, :])
        o_moba = _moba(*_moba_prep(proj, cos_full, sin_signed))
        h, hb3, route, route_t = _mix_out(o_gla, o_moba, w_o[l].astype(jnp.bfloat16), h,
                                          ln1_g[l][None, :], ln1_b[l][None, :], w_route, b_router)
        h, h_bf16 = _moe(h, hb3, route, route_t, w_gate[l].astype(jnp.bfloat16), w_up[l].astype(jnp.bfloat16),
                         w_down[l].astype(jnp.bfloat16), ln2_g[l][None, :], ln2_b[l][None, :])
    return h[None]
```
